```python
import math
import jax
import jax.numpy as jnp
from jax import lax
import numpy as np

D_MODEL = 2048
BATCH = 4
SEQ = 4096
DEPTH = 1

GRID_W = 64
CTX_LEN = 256
D_MIX = D_MODEL
D_RWKV = D_MIX // 2
RWKV_HEAD = 64
N_RWKV_HEADS = D_RWKV // RWKV_HEAD
DECAY_LORA = 64
AAA_LORA = 64
GATE_LORA = 160
D_S5 = D_MIX - D_RWKV
S5_GROUP = 16
N_S5_GROUPS = D_S5 // S5_GROUP
S5_STATE = 64
D_FF = 5632
N_MOD = 6
RMS_EPS = 1e-6
LNX_EPS = 64e-5
RWKV_SIZES = (D_RWKV, D_RWKV, D_RWKV, GATE_LORA, DECAY_LORA, DECAY_LORA, AAA_LORA, AAA_LORA)
RWKV_IN = 3 * D_RWKV + GATE_LORA + 2 * DECAY_LORA + 2 * AAA_LORA
N_IN = RWKV_IN + D_S5

kernel_name = "hybrid_rwkv7_s5_prefix_dit_block"


def _rms_norm(x, g):
    xf = x.astype(jnp.float32)
    y = xf * lax.rsqrt(jnp.mean(xf * xf, axis=-1, keepdims=True) + RMS_EPS)
    return (y * g.astype(jnp.float32)).astype(x.dtype)


def _modulate(h, shift, scale):
    return h * (1.0 + scale) + shift


def _neighbours(p, rows):
    b, t, ch = p.shape
    q = p.reshape(b, rows, t // rows, ch)
    q = jnp.pad(q, ((0, 0), (0, 0), (1, 1), (0, 0)))
    return q[:, :, :-2].reshape(b, t, ch), q[:, :, 2:].reshape(b, t, ch)


def _split_cols(z, sizes):
    out, start = [], 0
    for s in sizes:
        out.append(z[..., start:start + s])
        start += s
    return out


def _wkv7(r, decay, k, v, kk, a, s0, reverse, want_output):
    tm = lambda z: jnp.moveaxis(z, 1, 0)
    xs = (tm(decay), tm(k), tm(v), tm(kk), tm(kk * a))
    if want_output:
        xs = xs + (tm(r),)

    def step(s, inp):
        w_t, k_t, v_t, kk_t, b_t = inp[:5]
        sa = jnp.einsum('bhvk,bhk->bhv', s, kk_t)
        s = s * w_t[:, :, None, :] - sa[..., None] * b_t[:, :, None, :] + v_t[..., None] * k_t[:, :, None, :]
        y = jnp.einsum('bhvk,bhk->bhv', s, inp[5]) if want_output else None
        return s, y

    s_fin, ys = lax.scan(step, s0, xs, reverse=reverse)
    return (jnp.moveaxis(ys, 0, 1) if want_output else None), s_fin


def _zoh(a_re, a_im, log_step, b_re, b_im):
    f32 = jnp.float32
    a_re, a_im = a_re.astype(f32), a_im.astype(f32)
    b_re, b_im = b_re.astype(f32), b_im.astype(f32)
    dt = jnp.exp(log_step.astype(f32))[:, None]
    mag = jnp.exp(a_re * dt)
    ang = a_im * dt
    lam_re, lam_im = mag * jnp.cos(ang), mag * jnp.sin(ang)
    den = a_re * a_re + a_im * a_im
    nr = lam_re - 1.0
    f_re = (nr * a_re + lam_im * a_im) / den
    f_im = (lam_im * a_re - nr * a_im) / den
    bb_re = f_re[..., None] * b_re - f_im[..., None] * b_im
    bb_im = f_re[..., None] * b_im + f_im[..., None] * b_re
    return lam_re, lam_im, bb_re, bb_im


def _s5_scan(bu_re, bu_im, lam_re, lam_im, s0, reverse):
    a_re = jnp.broadcast_to(lam_re, bu_re.shape)
    a_im = jnp.broadcast_to(lam_im, bu_im.shape)

    def combine(e1, e2):
        a1r, a1i, b1r, b1i = e1
        a2r, a2i, b2r, b2i = e2
        return (a2r * a1r - a2i * a1i, a2r * a1i + a2i * a1r,
                a2r * b1r - a2i * b1i + b2r, a2r * b1i + a2i * b1r + b2i)

    ar, ai, sr, si = lax.associative_scan(combine, (a_re, a_im, bu_re, bu_im), reverse=reverse, axis=1)
    if s0 is not None:
        s0r, s0i = s0[0][:, None], s0[1][:, None]
        sr, si = sr + ar * s0r - ai * s0i, si + ar * s0i + ai * s0r
    return sr, si


def _token_mixer(h, rows, lp, init, want_output, want_states):
    f32 = jnp.float32
    bsz, t, _ = h.shape
    p = jnp.einsum('btd,dn->btn', h, lp['w_in'])
    q = p[..., :RWKV_IN]
    prev, nxt = _neighbours(q, rows)
    mu = lp['shift_mu']
    q = (q + mu[0] * (prev - q) + mu[1] * (nxt - q)).astype(f32)
    u = p[..., RWKV_IN:].astype(f32).reshape(bsz, t, N_S5_GROUPS, S5_GROUP)

    r, k, v, gd, wd_f, wd_b, ad_f, ad_b = _split_cols(q, RWKV_SIZES)
    heads = lambda z: z.reshape(bsz, t, N_RWKV_HEADS, RWKV_HEAD)
    kk = heads(k * lp['rwkv_k_k'])
    kk = kk / jnp.maximum(jnp.sqrt(jnp.sum(kk * kk, axis=-1, keepdims=True)), 1e-12)
    rh, vh = heads(r), heads(v)
    rwkv_y, rwkv_fin = [], []
    for d, (wd, ad) in enumerate(((wd_f, ad_f), (wd_b, ad_b))):
        w_log = -jax.nn.softplus(-(lp['rwkv_w0'][d] + jnp.tanh(wd) @ lp['rwkv_w_up'][d])) - 0.5
        decay = jnp.exp(-jnp.exp(w_log))
        a = jax.nn.sigmoid(lp['rwkv_a0'][d] + ad @ lp['rwkv_a_up'][d])
        kd = k * (1.0 + (a - 1.0) * lp['rwkv_k_a'])
        if init is None:
            s0 = jnp.zeros((bsz, N_RWKV_HEADS, RWKV_HEAD, RWKV_HEAD), f32)
        else:
            s0 = init['rwkv'][d]
        y, s_fin = _wkv7(rh, heads(decay), heads(kd), vh, kk, heads(a), s0, d == 1, want_output)
        rwkv_y.append(y)
        rwkv_fin.append(s_fin)

    s5_y, s5_fin = [], []
    for d in range(2):
        lam_re, lam_im, bb_re, bb_im = _zoh(lp['s5_a_re'][d], lp['s5_a_im'][d], lp['s5_log_step'][d],
                                            lp['s5_b_re'][d], lp['s5_b_im'][d])
        bu_re = jnp.einsum('gph,btgh->btgp', bb_re, u)
        bu_im = jnp.einsum('gph,btgh->btgp', bb_im, u)
        s0 = None if init is None else init['s5'][d]
        sr, si = _s5_scan(bu_re, bu_im, lam_re, lam_im, s0, d == 1)
        if want_output:
            s5_y.append(jnp.einsum('ghp,btgp->btgh', lp['s5_c_re'][d], sr)
                        - jnp.einsum('ghp,btgp->btgh', lp['s5_c_im'][d], si))
        if want_states:
            idx = -1 if d == 0 else 0
            s5_fin.append((sr[:, idx], si[:, idx]))

    states = {'rwkv': rwkv_fin, 's5': s5_fin} if want_states else None
    if not want_output:
        return None, states

    yh = rwkv_y[0] + rwkv_y[1]
    mean = jnp.mean(yh, axis=-1, keepdims=True)
    var = jnp.mean(jnp.square(yh - mean), axis=-1, keepdims=True)
    yn = ((yh - mean) * lax.rsqrt(var + LNX_EPS)).reshape(bsz, t, D_RWKV) * lp['lnx_w'] + lp['lnx_b']
    bonus = (jnp.sum(rh * heads(k) * lp['rwkv_r_k'], axis=-1, keepdims=True) * vh).reshape(bsz, t, D_RWKV)
    g = jax.nn.sigmoid(gd) @ lp['rwkv_g_up']
    o_rwkv = (yn + bonus) * g

    y5 = (s5_y[0] + s5_y[1] + lp['s5_d'].reshape(N_S5_GROUPS, S5_GROUP) * u).reshape(bsz, t, D_S5)
    z = jax.nn.gelu(y5)
    o_s5 = z * jax.nn.sigmoid(z @ lp['s5_glu_w'] + lp['s5_glu_b'])

    o = jnp.concatenate([o_rwkv, o_s5], axis=-1).astype(h.dtype)
    return jnp.einsum('btm,md->btd', o, lp['w_out']), states


def _conv_ffn(h, rows, w_up, conv_w, conv_b, w_down):
    up = jnp.einsum('btd,df->btf', h, w_up)
    gate, val = up[..., :D_FF], up[..., D_FF:]
    prev, nxt = _neighbours(gate, rows)
    gate = conv_w[0] * prev + conv_w[1] * gate + conv_w[2] * nxt + conv_b
    return jnp.einsum('btf,fd->btd', jax.nn.gelu(gate) * val, w_down)


def setup_inputs(seed: int = 0) -> dict:
    key = jax.random.key(seed)
    ks = jax.random.split(key, 40)
    f32 = jnp.float32
    L, G, P, H, N = DEPTH, N_S5_GROUPS, S5_STATE, N_RWKV_HEADS, RWKV_HEAD

    def nrm(i, shape, scale):
        return jax.random.normal(ks[i], shape, f32) * scale

    return {
        'x': nrm(0, (BATCH, SEQ, D_MODEL), 1.0),
        'c': nrm(1, (BATCH, D_MODEL), 1.0),
        'ctx': nrm(2, (BATCH, CTX_LEN, D_MODEL), 1.0),
        'c_ctx': nrm(3, (D_MODEL,), 1.0),
        'mod_w': nrm(4, (L, D_MODEL, N_MOD * D_MODEL), 0.5 * D_MODEL ** -0.5),
        'mod_b': nrm(5, (L, N_MOD * D_MODEL), 0.02),
        'norm_mix_g': 1.0 + nrm(6, (L, D_MODEL), 0.02),
        'w_in': nrm(7, (L, D_MODEL, N_IN), D_MODEL ** -0.5),
        'shift_mu': jax.random.uniform(ks[8], (L, 2, RWKV_IN), f32, 0.0, 0.5),
        'rwkv_w0': jax.random.uniform(ks[9], (L, 2, D_RWKV), f32, -6.5, -1.5),
        'rwkv_w_up': nrm(10, (L, 2, DECAY_LORA, D_RWKV), 0.5 * DECAY_LORA ** -0.5),
        'rwkv_a0': nrm(11, (L, 2, D_RWKV), 0.1),
        'rwkv_a_up': nrm(12, (L, 2, AAA_LORA, D_RWKV), 0.5 * AAA_LORA ** -0.5),
        'rwkv_g_up': nrm(13, (L, GATE_LORA, D_RWKV), GATE_LORA ** -0.5),
        'rwkv_k_k': 0.85 + nrm(14, (L, D_RWKV), 0.02),
        'rwkv_k_a': 1.0 + nrm(15, (L, D_RWKV), 0.02),
        'rwkv_r_k': nrm(16, (L, H, N), 0.1),
        'lnx_w': 1.0 + nrm(17, (L, D_RWKV), 0.02),
        'lnx_b': nrm(18, (L, D_RWKV), 0.02),
        's5_a_re': -0.5 + nrm(19, (L, 2, G, P), 0.01),
        's5_a_im': jnp.pi * jnp.arange(P, dtype=f32) + nrm(20, (L, 2, G, P), 0.01),
        's5_log_step': jax.random.uniform(ks[21], (L, 2, G), f32, math.log(1e-3), math.log(1e-1)),
        's5_b_re': nrm(22, (L, 2, G, P, S5_GROUP), (2 * S5_GROUP) ** -0.5),
        's5_b_im': nrm(23, (L, 2, G, P, S5_GROUP), (2 * S5_GROUP) ** -0.5),
        's5_c_re': nrm(24, (L, 2, G, S5_GROUP, P), P ** -0.5),
        's5_c_im': nrm(25, (L, 2, G, S5_GROUP, P), P ** -0.5),
        's5_d': nrm(26, (L, D_S5), 1.0),
        's5_glu_w': nrm(27, (L, D_S5, D_S5), D_S5 ** -0.5),
        's5_glu_b': nrm(28, (L, D_S5), 0.02),
        'w_out': nrm(29, (L, D_MIX, D_MODEL), D_MIX ** -0.5),
        'norm_ffn_g': 1.0 + nrm(30, (L, D_MODEL), 0.02),
        'ffn_w_up': nrm(31, (L, D_MODEL, 2 * D_FF), D_MODEL ** -0.5),
        'ffn_conv_w': nrm(32, (L, 3, D_FF), 3.0 ** -0.5),
        'ffn_conv_b': nrm(33, (L, D_FF), 0.02),
        'ffn_w_down': nrm(34, (L, D_FF, D_MODEL), D_FF ** -0.5),
        'final_norm_g': 1.0 + nrm(35, (D_MODEL,), 0.02),
    }


def reference(x, c, ctx, c_ctx, mod_w, mod_b, norm_mix_g, w_in, shift_mu, rwkv_w0, rwkv_w_up, rwkv_a0,
              rwkv_a_up, rwkv_g_up, rwkv_k_k, rwkv_k_a, rwkv_r_k, lnx_w, lnx_b, s5_a_re, s5_a_im,
              s5_log_step, s5_b_re, s5_b_im, s5_c_re, s5_c_im, s5_d, s5_glu_w, s5_glu_b, w_out,
              norm_ffn_g, ffn_w_up, ffn_conv_w, ffn_conv_b, ffn_w_down, final_norm_g):
    rows = x.shape[1] // GRID_W
    for i in range(DEPTH):
        lp = {
            'w_in': w_in[i], 'shift_mu': shift_mu[i],
            'rwkv_w0': rwkv_w0[i], 'rwkv_w_up': rwkv_w_up[i], 'rwkv_a0': rwkv_a0[i], 'rwkv_a_up': rwkv_a_up[i],
            'rwkv_g_up': rwkv_g_up[i], 'rwkv_k_k': rwkv_k_k[i], 'rwkv_k_a': rwkv_k_a[i], 'rwkv_r_k': rwkv_r_k[i],
            'lnx_w': lnx_w[i], 'lnx_b': lnx_b[i],
            's5_a_re': s5_a_re[i], 's5_a_im': s5_a_im[i], 's5_log_step': s5_log_step[i],
            's5_b_re': s5_b_re[i], 's5_b_im': s5_b_im[i], 's5_c_re': s5_c_re[i], 's5_c_im': s5_c_im[i],
            's5_d': s5_d[i], 's5_glu_w': s5_glu_w[i], 's5_glu_b': s5_glu_b[i], 'w_out': w_out[i],
        }
        last = i == DEPTH - 1
        mod_x = (jax.nn.silu(c) @ mod_w[i] + mod_b[i])[:, None, :]
        mod_c = (jax.nn.silu(c_ctx)[None] @ mod_w[i] + mod_b[i])[:, None, :]
        sh_x, sc_x, gt_x, shf_x, scf_x, gtf_x = jnp.split(mod_x, N_MOD, axis=-1)
        sh_c, sc_c, gt_c, shf_c, scf_c, gtf_c = jnp.split(mod_c, N_MOD, axis=-1)

        hc = _modulate(_rms_norm(ctx, norm_mix_g[i]), sh_c, sc_c)
        ctx_mix, ctx_states = _token_mixer(hc, 1, lp, None, not last, True)

        hx = _modulate(_rms_norm(x, norm_mix_g[i]), sh_x, sc_x)
        x_mix, _ = _token_mixer(hx, rows, lp, ctx_states, True, False)
        x = x + gt_x * x_mix
        hx = _modulate(_rms_norm(x, norm_ffn_g[i]), shf_x, scf_x)
        x = x + gtf_x * _conv_ffn(hx, rows, ffn_w_up[i], ffn_conv_w[i], ffn_conv_b[i], ffn_w_down[i])

        if not last:
            ctx = ctx + gt_c * ctx_mix
            hc = _modulate(_rms_norm(ctx, norm_ffn_g[i]), shf_c, scf_c)
            ctx = ctx + gtf_c * _conv_ffn(hc, 1, ffn_w_up[i], ffn_conv_w[i], ffn_conv_b[i], ffn_w_down[i])
    return _rms_norm(x, final_norm_g)
```

```python
import functools
import math

import jax
import jax.numpy as jnp
from jax import lax
from jax.experimental import pallas as pl
from jax.experimental.pallas import tpu as pltpu

F32 = jnp.float32
BF16 = jnp.bfloat16

HEAD = 64
GRID_W = 64
S5_GROUP = 16
S5_STATE = 64
RMS_EPS = 1e-6
LNX_EPS = 64e-5
LANE = 128
MXU = 256
HEADS_PER_GROUP = MXU // HEAD
WKV_CHUNK = 64
S5_CHUNK = 16
TOK_TILE = 256
FFN_TOK_TILE = 512
FFN_F_TILE = 512
VMEM_LIMIT = 56 * 1024 * 1024


def _cparams(sem):
    return pltpu.CompilerParams(dimension_semantics=sem, vmem_limit_bytes=VMEM_LIMIT)


def _round_up(n, m):
    return (n + m - 1) // m * m


def _split(a):
    hi = a.astype(BF16)
    lo = (a - hi.astype(F32)).astype(BF16)
    return hi, lo


def _mm(a, b, dims=(((1,), (0,)), ((), ())), split_a=False, split_b=False):
    dot = functools.partial(lax.dot_general, dimension_numbers=dims, preferred_element_type=F32)
    if not split_a and not split_b:
        return dot(a.astype(BF16), b.astype(BF16))
    a_hi, a_lo = _split(a) if split_a else (a.astype(BF16), None)
    b_hi, b_lo = _split(b) if split_b else (b.astype(BF16), None)
    out = dot(a_hi, b_hi)
    if a_lo is not None:
        out = out + dot(a_lo, b_hi)
    if b_lo is not None:
        out = out + dot(a_hi, b_lo)
    return out


_NT = (((1,), (1,)), ((), ()))
_TN = (((0,), (0,)), ((), ()))


def _sigmoid(x):
    return 1.0 / (1.0 + jnp.exp(-x))


def _gelu(x):
    return 0.5 * x * (1.0 + jnp.tanh(math.sqrt(2.0 / math.pi) * (x + 0.044715 * (x * x * x))))


def _head_ones():
    r = lax.broadcasted_iota(jnp.int32, (MXU, MXU), 0) // HEAD
    c = lax.broadcasted_iota(jnp.int32, (MXU, MXU), 1) // HEAD
    return jnp.where(r == c, 1.0, 0.0).astype(BF16)


def _head_sum(x, ones):
    parts = []
    for g in range(x.shape[1] // MXU):
        parts.append(_mm(x[:, g * MXU:(g + 1) * MXU], ones, split_a=True))
    return parts[0] if len(parts) == 1 else jnp.concatenate(parts, axis=1)


def _shift_rows(x, whole_tile=False):
    n = x.shape[0]
    row = lax.broadcasted_iota(jnp.int32, (n, 1), 0)
    pos = row & (GRID_W - 1)
    first = jnp.where(whole_tile, row, pos) == 0
    last = jnp.where(whole_tile, row - (n - 1), pos - (GRID_W - 1)) == 0
    prev = jnp.where(first, 0.0, pltpu.roll(x, 1, 0))
    nxt = jnp.where(last, 0.0, pltpu.roll(x, n - 1, 0))
    return prev, nxt


def _mod_kernel(c_ref, w_ref, b_ref, o_ref):
    c = c_ref[...]
    s = c * _sigmoid(c)
    o_ref[...] = lax.dot_general(s, w_ref[...], (((1,), (0,)), ((), ())), precision=lax.Precision.HIGHEST,
                                 preferred_element_type=F32) + b_ref[...]


def _mod(c_all, mod_w, mod_b):
    rows, d = c_all.shape
    n = mod_w.shape[1]
    tn = 1024
    return pl.pallas_call(
        _mod_kernel,
        grid=(n // tn,),
        in_specs=[pl.BlockSpec((rows, d), lambda j: (0, 0)),
                  pl.BlockSpec((d, tn), lambda j: (0, j)),
                  pl.BlockSpec((1, tn), lambda j: (0, j))],
        out_specs=pl.BlockSpec((rows, tn), lambda j: (0, j)),
        out_shape=jax.ShapeDtypeStruct((rows, n), F32),
        compiler_params=_cparams(("parallel",)),
        name="mod",
    )(c_all, mod_w, mod_b)


def _inproj_kernel(x_ref, mod_ref, g_ref, w_ref, o_ref):
    x = x_ref[...]
    ms = jnp.mean(x * x, axis=-1, keepdims=True)
    h = x * lax.rsqrt(ms + RMS_EPS) * g_ref[...]
    h = h * (1.0 + mod_ref[1:2, :]) + mod_ref[0:1, :]
    o_ref[...] = jnp.dot(h.astype(BF16), w_ref[...], preferred_element_type=F32)


def _inproj(xc, mod3, g, w_packed, n_batch):
    b, tc, d = xc.shape
    npk = w_packed.shape[1]
    tn = npk // 4
    nt = tc // TOK_TILE
    return pl.pallas_call(
        _inproj_kernel,
        grid=(npk // tn, b, nt),
        in_specs=[pl.BlockSpec((None, TOK_TILE, d), lambda j, bi, i: (bi, i, 0)),
                  pl.BlockSpec((None, 6, d), lambda j, bi, i: (jnp.where(i == 0, n_batch, bi), 0, 0)),
                  pl.BlockSpec((1, d), lambda j, bi, i: (0, 0)),
                  pl.BlockSpec((d, tn), lambda j, bi, i: (0, j))],
        out_specs=pl.BlockSpec((None, TOK_TILE, tn), lambda j, bi, i: (bi, i, j)),
        out_shape=jax.ShapeDtypeStruct((b, tc, npk), F32),
        compiler_params=_cparams(("parallel", "parallel", "parallel")),
        name="inproj",
    )(xc, mod3, g, w_packed)


def _prep_kernel(dims, p_ref, mu_ref, w0_ref, a0_ref, kk_w_ref, ka_ref, rk_ref, wup_ref, aup_ref, gup_ref,
                 r_ref, v_ref, kk_ref, lw_ref, kd_ref, bb_ref, bonus_ref, g_ref):
    dr, glp, wlp, alp = dims
    nq = 3 * dr + glp + wlp + alp
    q = p_ref[:, :nq]
    prev, nxt = _shift_rows(q, whole_tile=pl.program_id(1) == 0)
    q = q + mu_ref[0:1, :] * (prev - q) + mu_ref[1:2, :] * (nxt - q)
    r = q[:, 0:dr]
    k = q[:, dr:2 * dr]
    v = q[:, 2 * dr:3 * dr]
    o = 3 * dr
    gd = q[:, o:o + glp]
    wd = q[:, o + glp:o + glp + wlp]
    ad = q[:, o + glp + wlp:o + glp + wlp + alp]

    ones = _head_ones()
    kraw = k * kk_w_ref[...]
    kk = kraw / jnp.maximum(jnp.sqrt(_head_sum(kraw * kraw, ones)), 1e-12)
    lora_w = _mm(jnp.tanh(wd), wup_ref[...], split_a=True)
    lora_a = _mm(ad, aup_ref[...], split_a=True)
    r_ref[...] = r
    v_ref[...] = v
    kk_ref[...] = kk
    for d in range(2):
        z = w0_ref[d:d + 1, :] + lora_w[:, d * dr:(d + 1) * dr]
        nz = -z
        softplus = jnp.maximum(nz, 0.0) + jnp.log1p(jnp.exp(-jnp.abs(nz)))
        w_log = -softplus - 0.5
        lw_ref[d] = -jnp.exp(w_log)
        a = _sigmoid(a0_ref[d:d + 1, :] + lora_a[:, d * dr:(d + 1) * dr])
        kd_ref[d] = k * (1.0 + (a - 1.0) * ka_ref[...])
        bb_ref[d] = kk * a
    bonus_ref[...] = _head_sum(r * k * rk_ref[...], ones) * v
    g_ref[...] = _mm(_sigmoid(gd), gup_ref[...])


def _prep(p, mu_p, w0, a0, k_k, k_a, r_k, wup_c, aup_c, gup_p, dims):
    b, tc, npk = p.shape
    dr, glp, wlp, alp = dims
    nq = 3 * dr + glp + wlp + alp
    nt = tc // TOK_TILE
    tok = lambda bi, i: (bi, i, 0)
    tok2 = lambda bi, i: (0, bi, i, 0)
    const = lambda bi, i: (0, 0)
    one = jax.ShapeDtypeStruct((b, tc, dr), F32)
    two = jax.ShapeDtypeStruct((2, b, tc, dr), F32)
    spec1 = pl.BlockSpec((None, TOK_TILE, dr), tok)
    spec2 = pl.BlockSpec((2, None, TOK_TILE, dr), tok2)
    return pl.pallas_call(
        functools.partial(_prep_kernel, dims),
        grid=(b, nt),
        in_specs=[pl.BlockSpec((None, TOK_TILE, npk), tok),
                  pl.BlockSpec((2, nq), const),
                  pl.BlockSpec((2, dr), const), pl.BlockSpec((2, dr), const),
                  pl.BlockSpec((1, dr), const), pl.BlockSpec((1, dr), const), pl.BlockSpec((1, dr), const),
                  pl.BlockSpec((wlp, 2 * dr), const), pl.BlockSpec((alp, 2 * dr), const),
                  pl.BlockSpec((glp, dr), const)],
        out_specs=[spec1, spec1, spec1, spec2, spec2, spec2, spec1, spec1],
        out_shape=[one, one, one, two, two, two, one, one],
        compiler_params=_cparams(("parallel", "parallel")),
        name="prep",
    )(p, mu_p, w0, a0, k_k, k_a, r_k, wup_c, aup_c, gup_p)


def _bd(x, bdmask):
    return jnp.where(bdmask, jnp.concatenate([x] * HEADS_PER_GROUP, axis=0), 0.0)


def _wkv_kernel(n_ctx_chunks, r_ref, v_ref, kk_ref, lw_ref, kd_ref, bb_ref, y_ref, s_ref):
    d = pl.program_id(1)
    i = pl.program_id(2)
    L = WKV_CHUNK
    n_groups = r_ref.shape[1] // MXU

    @pl.when(i == 0)
    def _():
        s_ref[...] = jnp.zeros_like(s_ref)

    fwd = d == 0
    sgn = 1 - 2 * d
    row = lax.broadcasted_iota(jnp.int32, (L, L), 0)
    col = lax.broadcasted_iota(jnp.int32, (L, L), 1)
    tri = jnp.where((row - col) * sgn >= 0, 1.0, 0.0).astype(BF16)
    lw = lw_ref[...]
    lw_hi = lw.astype(BF16)
    lw_mid = (lw - lw_hi.astype(F32))
    lw_lo = lw_mid - lw_mid.astype(BF16).astype(F32)
    dotf = functools.partial(jnp.dot, preferred_element_type=F32)
    cum = dotf(tri, lw_hi) + dotf(tri, lw_mid.astype(BF16)) + dotf(tri, lw_lo.astype(BF16))

    t_idx = lax.broadcasted_iota(jnp.int32, (L, MXU), 0)
    s_idx = lax.broadcasted_iota(jnp.int32, (L, MXU), 1) % HEAD
    strict = (t_idx - s_idx) * sgn > 0
    incl = (t_idx - s_idx) * sgn >= 0
    eye_sbs = jnp.where(t_idx == s_idx, 1.0, 0.0)
    br = lax.broadcasted_iota(jnp.int32, (MXU, MXU), 0)
    bc = lax.broadcasted_iota(jnp.int32, (MXU, MXU), 1)
    bdmask = (br // HEAD) == (bc // HEAD)
    eye_bd = jnp.where(br == bc, 1.0, 0.0)

    for g in range(n_groups):
        sl = slice(g * MXU, (g + 1) * MXU)
        c = cum[:, sl]
        lwg = lw[:, sl]
        p_in = jnp.exp(c)
        p_inv = jnp.exp(-c)
        kt = kk_ref[:, sl] * jnp.exp(c - lwg)
        bt = bb_ref[:, sl] * p_inv
        kdt = kd_ref[:, sl] * p_inv
        rt = r_ref[:, sl] * p_in
        vv = v_ref[:, sl]
        p_end = jnp.exp(jnp.where(fwd, c[L - 1:L, :], c[0:1, :]))

        a_all = _mm(jnp.concatenate([kt, rt], axis=0),
                    jnp.concatenate([_bd(bt, bdmask), _bd(kdt, bdmask)], axis=0), _NT)
        a_ab = jnp.where(strict, a_all[:L, :MXU], 0.0)
        a_ak = jnp.where(strict, a_all[:L, MXU:], 0.0)
        a_rb = jnp.where(incl, a_all[L:, :MXU], 0.0)
        a_rk = jnp.where(incl, a_all[L:, MXU:], 0.0)

        npow = a_ab
        tinv = eye_sbs - a_ab
        for _ in range(int(math.log2(L)) - 1):
            npow = _mm(npow, _bd(npow, bdmask))
            tinv = tinv + _mm(tinv, _bd(npow, bdmask))

        av = _mm(jnp.concatenate([a_ak, a_rk], axis=0), _bd(vv, bdmask))
        wu = _mm(tinv, jnp.concatenate([_bd(kt, bdmask), _bd(av[:L], bdmask)], axis=1))
        wm, um = wu[:, :MXU], wu[:, MXU:]
        qy = _mm(a_rb, jnp.concatenate([_bd(wm, bdmask), _bd(um, bdmask)], axis=1))
        qm = rt - qy[:, :MXU]
        yloc = av[L:] - qy[:, MXU:]
        gmat = jnp.where(bdmask, eye_bd - _mm(wm, bt, _TN), 0.0) * p_end
        hmat = jnp.where(bdmask, _mm(jnp.concatenate([vv, -um], axis=0),
                                     jnp.concatenate([kdt, bt], axis=0), _TN), 0.0) * p_end

        s0 = s_ref[g]
        y_ref[:, sl] = _mm(qm, s0, _NT, split_b=True) + yloc
        s_ref[g] = _mm(s0, gmat, split_a=True) + hmat


def _wkv(r, v, kk, lw, kd, bb, n_ctx_chunks):
    b, tc, dr = r.shape
    nch = tc // WKV_CHUNK

    def chunk(d, i):
        bwd = jnp.where(i < n_ctx_chunks, n_ctx_chunks - 1 - i, nch + n_ctx_chunks - 1 - i)
        return jnp.where(d == 0, i, bwd)

    one = pl.BlockSpec((None, WKV_CHUNK, dr), lambda bi, d, i: (bi, chunk(d, i), 0))
    two = pl.BlockSpec((None, None, WKV_CHUNK, dr), lambda bi, d, i: (d, bi, chunk(d, i), 0))
    return pl.pallas_call(
        functools.partial(_wkv_kernel, n_ctx_chunks),
        grid=(b, 2, nch),
        in_specs=[one, one, one, two, two, two],
        out_specs=two,
        out_shape=jax.ShapeDtypeStruct((2, b, tc, dr), F32),
        scratch_shapes=[pltpu.VMEM((dr // MXU, MXU, MXU), F32)],
        compiler_params=_cparams(("parallel", "parallel", "arbitrary")),
        name="wkv",
    )(r, v, kk, lw, kd, bb)


def _s5w_kernel(ar_ref, ai_ref, ls_ref, br_ref, bi_ref, ctr_ref, cti_ref, cr_ref, ci_ref,
                lbr_ref, lbi_ref, clr_ref, cli_ref, kt_ref, l16r_ref, l16i_ref):
    ar, ai = ar_ref[...], ai_ref[...]
    dt = jnp.exp(ls_ref[...])
    mag = jnp.exp(ar * dt)
    ang = ai * dt
    lr, li = mag * jnp.cos(ang), mag * jnp.sin(ang)
    den = ar * ar + ai * ai
    nr = lr - 1.0
    f_re = (nr * ar + li * ai) / den
    f_im = (li * ar - nr * ai) / den
    b_re, b_im = br_ref[...], bi_ref[...]
    bb_re = f_re * b_re - f_im * b_im
    bb_im = f_re * b_im + f_im * b_re
    n = lax.broadcasted_iota(jnp.int32, ar.shape, ar.ndim - 1) // S5_GROUP
    pr, pi = jnp.ones_like(ar), jnp.zeros_like(ar)
    sr, si = lr, li
    for bit in range(int(math.log2(S5_CHUNK))):
        take = ((n >> bit) & 1) == 1
        pr, pi = jnp.where(take, pr * sr - pi * si, pr), jnp.where(take, pr * si + pi * sr, pi)
        sr, si = sr * sr - si * si, 2.0 * sr * si
    l16r_ref[...] = sr
    l16i_ref[...] = si
    lb_re = pr * bb_re - pi * bb_im
    lb_im = pr * bb_im + pi * bb_re
    lbr_ref[...] = lb_re
    lbi_ref[...] = lb_im
    c_re, c_im = ctr_ref[...], cti_ref[...]
    c1r = c_re * lr - c_im * li
    c1i = c_re * li + c_im * lr
    clr_ref[...] = c1r * pr - c1i * pi
    cli_ref[...] = -(c1r * pi + c1i * pr)
    hp = lax.Precision.HIGHEST
    kt_ref[...] = (jnp.einsum('ghp,gpn->ghn', cr_ref[...], lb_re, precision=hp, preferred_element_type=F32)
                   - jnp.einsum('ghp,gpn->ghn', ci_ref[...], lb_im, precision=hp, preferred_element_type=F32))


def _s5w(ar_t, ai_t, ls_t, br_t, bi_t, ctr_t, cti_t, c_re, c_im):
    _, g, p, w = ar_t.shape
    gb = 8
    big = pl.BlockSpec((None, gb, p, w), lambda d, j: (d, j, 0, 0))
    cs = pl.BlockSpec((None, gb, S5_GROUP, p), lambda d, j: (d, j, 0, 0))
    ks = pl.BlockSpec((None, gb, S5_GROUP, w), lambda d, j: (d, j, 0, 0))
    big_shape = jax.ShapeDtypeStruct((2, g, p, w), F32)
    return pl.pallas_call(
        _s5w_kernel,
        grid=(2, g // gb),
        in_specs=[big] * 7 + [cs, cs],
        out_specs=[big, big, big, big, ks, big, big],
        out_shape=[big_shape] * 4 + [jax.ShapeDtypeStruct((2, g, S5_GROUP, w), F32)] + [big_shape] * 2,
        compiler_params=_cparams(("parallel", "parallel")),
        name="s5w",
    )(ar_t, ai_t, ls_t, br_t, bi_t, ctr_t, cti_t, c_re, c_im)


def _s5_kernel(n_ctx_chunks, nb, u_ref, wp_ref, mf_ref, mb_ref, nf_ref, nb_ref, l16_ref, d_ref, y_ref,
               e_ref, sf_ref, sb_ref):
    gb = u_ref.shape[0]
    lanes = gb * nb
    nck = e_ref.shape[1] // lanes
    for g in range(gb):
        e = _mm(u_ref[g], wp_ref[g], split_a=True, split_b=True)
        for b in range(nb):
            for k in range(4):
                e_ref[k, pl.ds(g * nb + b, nck, stride=lanes), :] = e[b * nck:(b + 1) * nck, k * LANE:(k + 1) * LANE]

    def lam_rows(k):
        return jnp.concatenate([jnp.broadcast_to(l16_ref[g, k:k + 1, :], (nb, LANE)) for g in range(gb)], axis=0)

    lfr, lfi, lbr, lbi = (lam_rows(k) for k in range(4))
    zero = jnp.zeros((lanes, LANE), F32)

    def step(s_ref, lr, li, slot0, c, carry):
        sr, si = carry
        base = pl.multiple_of(c * lanes, lanes)
        s_ref[0, pl.ds(base, lanes), :] = sr
        s_ref[1, pl.ds(base, lanes), :] = si
        er = e_ref[slot0, pl.ds(base, lanes), :]
        ei = e_ref[slot0 + 1, pl.ds(base, lanes), :]
        return (lr * sr - li * si + er, lr * si + li * sr + ei)

    fwd_step = functools.partial(step, sf_ref, lfr, lfi, 0)
    bwd_step = functools.partial(step, sb_ref, lbr, lbi, 2)
    lax.fori_loop(0, nck, fwd_step, (zero, zero))
    carry = lax.fori_loop(0, n_ctx_chunks, lambda k, cy: bwd_step(n_ctx_chunks - 1 - k, cy), (zero, zero))
    lax.fori_loop(0, nck - n_ctx_chunks, lambda k, cy: bwd_step(nck - 1 - k, cy), carry)

    def rows_of(s_ref, g):
        return jnp.concatenate(
            [jnp.concatenate([s_ref[k, pl.ds(g * nb + b, nck, stride=lanes), :] for k in range(2)], axis=1)
             for b in range(nb)], axis=0)

    for g in range(gb):
        u = u_ref[g]
        y = _mm(u, mf_ref[g] + mb_ref[g], split_a=True, split_b=True)
        y = y + _mm(rows_of(sf_ref, g), nf_ref[g], split_a=True, split_b=True)
        y = y + _mm(rows_of(sb_ref, g), nb_ref[g], split_a=True, split_b=True)
        y_ref[g] = y + u * d_ref[g]


def _s5(u_t, wp, mf, mb, nf, nbw, l16, d_t, n_batch, n_ctx_chunks):
    g, rows, w = u_t.shape
    nck = rows // n_batch
    gb = 4
    grp = lambda shape: pl.BlockSpec((gb,) + shape, lambda j: (j,) + (0,) * len(shape))
    return pl.pallas_call(
        functools.partial(_s5_kernel, n_ctx_chunks, n_batch),
        grid=(g // gb,),
        in_specs=[grp((rows, w)), grp((w, 4 * LANE)), grp((w, w)), grp((w, w)),
                  grp((2 * LANE, w)), grp((2 * LANE, w)), grp((4, LANE)), grp((1, w))],
        out_specs=grp((rows, w)),
        out_shape=jax.ShapeDtypeStruct((g, rows, w), F32),
        scratch_shapes=[pltpu.VMEM((4, nck * gb * n_batch, LANE), F32),
                        pltpu.VMEM((2, nck * gb * n_batch, LANE), F32),
                        pltpu.VMEM((2, nck * gb * n_batch, LANE), F32)],
        compiler_params=_cparams(("parallel",)),
        name="s5",
    )(u_t, wp, mf, mb, nf, nbw, l16, d_t)


def _merge_kernel(yf_ref, yb_ref, bonus_ref, g_ref, y5_ref, x_ref, mod_ref, lnw_ref, lnb_ref, gluw_ref, glub_ref,
                  wout_ref, gffn_ref, x1_ref, h2_ref):
    ones = _head_ones()
    yh = yf_ref[...] + yb_ref[...]
    mean = _head_sum(yh, ones) * (1.0 / HEAD)
    cen = yh - mean
    var = _head_sum(cen * cen, ones) * (1.0 / HEAD)
    yn = cen * lax.rsqrt(var + LNX_EPS) * lnw_ref[...] + lnb_ref[...]
    o_rwkv = (yn + bonus_ref[...]) * g_ref[...]
    z = _gelu(y5_ref[...])
    o_s5 = z * _sigmoid(_mm(z, gluw_ref[...]) + glub_ref[...])
    o = jnp.concatenate([o_rwkv, o_s5], axis=1)
    mix = _mm(o, wout_ref[...])
    x1 = x_ref[...] + mod_ref[2:3, :] * mix
    x1_ref[...] = x1
    ms = jnp.mean(x1 * x1, axis=-1, keepdims=True)
    h = x1 * lax.rsqrt(ms + RMS_EPS) * gffn_ref[...]
    h2_ref[...] = (h * (1.0 + mod_ref[4:5, :]) + mod_ref[3:4, :]).astype(BF16)


def _merge(y, bonus, g, y5, x, mod3, lnx_w, lnx_b, glu_w, glu_b, w_out, g_ffn, n_ctx_tiles):
    b, t, d = x.shape
    dr = bonus.shape[-1]
    ds = y5.shape[-1]
    nt = t // TOK_TILE
    off = lambda bi, i: (bi, i + n_ctx_tiles, 0)
    const = lambda bi, i: (0, 0)
    return pl.pallas_call(
        _merge_kernel,
        grid=(b, nt),
        in_specs=[pl.BlockSpec((None, None, TOK_TILE, dr), lambda bi, i: (0, bi, i + n_ctx_tiles, 0)),
                  pl.BlockSpec((None, None, TOK_TILE, dr), lambda bi, i: (1, bi, i + n_ctx_tiles, 0)),
                  pl.BlockSpec((None, TOK_TILE, dr), off),
                  pl.BlockSpec((None, TOK_TILE, dr), off),
                  pl.BlockSpec((None, TOK_TILE, ds), off),
                  pl.BlockSpec((None, TOK_TILE, d), lambda bi, i: (bi, i, 0)),
                  pl.BlockSpec((None, 6, d), lambda bi, i: (bi, 0, 0)),
                  pl.BlockSpec((1, dr), const), pl.BlockSpec((1, dr), const),
                  pl.BlockSpec((ds, ds), const), pl.BlockSpec((1, ds), const),
                  pl.BlockSpec((dr + ds, d), const), pl.BlockSpec((1, d), const)],
        out_specs=[pl.BlockSpec((None, TOK_TILE, d), lambda bi, i: (bi, i, 0)),
                   pl.BlockSpec((None, TOK_TILE, d), lambda bi, i: (bi, i, 0))],
        out_shape=[jax.ShapeDtypeStruct((b, t, d), F32), jax.ShapeDtypeStruct((b, t, d), BF16)],
        compiler_params=_cparams(("parallel", "parallel")),
        name="merge",
    )(y, y, bonus, g, y5, x, mod3, lnx_w, lnx_b, glu_w, glu_b, w_out, g_ffn)


def _ffn_kernel(h_ref, wg_ref, wv_ref, cw_ref, cb_ref, wd_ref, x1_ref, mod_ref, gfin_ref, o_ref, acc_ref):
    f = pl.program_id(1)

    @pl.when(f == 0)
    def _():
        acc_ref[...] = jnp.zeros_like(acc_ref)

    h = h_ref[...]
    gate = jnp.dot(h, wg_ref[...], preferred_element_type=F32)
    val = jnp.dot(h, wv_ref[...], preferred_element_type=F32)
    prev, nxt = _shift_rows(gate)
    gate = cw_ref[0:1, :] * prev + cw_ref[1:2, :] * gate + cw_ref[2:3, :] * nxt + cb_ref[...]
    act = (_gelu(gate) * val).astype(BF16)
    acc_ref[...] += jnp.dot(act, wd_ref[...], preferred_element_type=F32)

    @pl.when(f == pl.num_programs(1) - 1)
    def _():
        x2 = x1_ref[...] + mod_ref[5:6, :] * acc_ref[...]
        ms = jnp.mean(x2 * x2, axis=-1, keepdims=True)
        o_ref[...] = x2 * lax.rsqrt(ms + RMS_EPS) * gfin_ref[...]


def _ffn(h2, w_up, conv_w, conv_b, w_down, x1, mod3, g_fin):
    b, t, d = x1.shape
    ff = w_down.shape[0]
    tm = min(FFN_TOK_TILE, t)
    tf = min(FFN_F_TILE, ff)
    nt, nf = t // tm, ff // tf
    return pl.pallas_call(
        _ffn_kernel,
        grid=(b * nt, nf),
        in_specs=[pl.BlockSpec((None, tm, d), lambda i, f: (i // nt, i % nt, 0)),
                  pl.BlockSpec((d, tf), lambda i, f: (0, f)),
                  pl.BlockSpec((d, tf), lambda i, f: (0, nf + f)),
                  pl.BlockSpec((3, tf), lambda i, f: (0, f)),
                  pl.BlockSpec((1, tf), lambda i, f: (0, f)),
                  pl.BlockSpec((tf, d), lambda i, f: (f, 0)),
                  pl.BlockSpec((None, tm, d), lambda i, f: (i // nt, i % nt, 0)),
                  pl.BlockSpec((None, 6, d), lambda i, f: (i // nt, 0, 0)),
                  pl.BlockSpec((1, d), lambda i, f: (0, 0))],
        out_specs=pl.BlockSpec((None, tm, d), lambda i, f: (i // nt, i % nt, 0)),
        out_shape=jax.ShapeDtypeStruct((b, t, d), F32),
        scratch_shapes=[pltpu.VMEM((tm, d), F32)],
        compiler_params=_cparams(("parallel", "arbitrary")),
        name="ffn",
    )(h2, w_up, w_up, conv_w, conv_b, w_down, x1, mod3, g_fin)


def _pad_cols(a, n):
    return jnp.pad(a, [(0, 0)] * (a.ndim - 1) + [(0, n - a.shape[-1])])


def _pack_in(a, dr, sizes, pads):
    gl, dl, al = sizes
    glp, wlp, alp = pads
    o = 3 * dr
    parts = [a[..., :o], _pad_cols(a[..., o:o + gl], glp), _pad_cols(a[..., o + gl:o + gl + 2 * dl], wlp),
             _pad_cols(a[..., o + gl + 2 * dl:o + gl + 2 * dl + 2 * al], alp), a[..., o + gl + 2 * dl + 2 * al:]]
    return jnp.concatenate(parts, axis=-1)


def _lora_up(w, rows):
    _, r, dr = w.shape
    out = jnp.zeros((rows, 2 * dr), w.dtype)
    out = out.at[0:r, 0:dr].set(w[0])
    out = out.at[r:2 * r, dr:2 * dr].set(w[1])
    return out


def _s5_tables(lb_re, lb_im, cl_re, cl_im, kt, l16r, l16i):
    _, g, p, w = lb_re.shape
    n = S5_CHUNK
    hpad = LANE - p

    def rows_sj(a, flip):
        a = a.reshape(g, p, n, S5_GROUP)
        if flip:
            a = a[:, :, ::-1, :]
        return a.transpose(0, 2, 3, 1).reshape(g, w, p)

    def padp(a):
        return jnp.pad(a, [(0, 0), (0, 0), (0, hpad)])

    wp = jnp.concatenate([padp(rows_sj(lb_re[0], True)), padp(rows_sj(lb_im[0], True)),
                          padp(rows_sj(lb_re[1], False)), padp(rows_sj(lb_im[1], False))], axis=-1)

    def state_rows(re, im, flip):
        def one(a):
            a = a.reshape(g, p, n, S5_GROUP)
            if flip:
                a = a[:, :, ::-1, :]
            return jnp.pad(a.reshape(g, p, w), [(0, 0), (0, hpad), (0, 0)])
        return jnp.concatenate([one(re), one(im)], axis=1)

    nf = state_rows(cl_re[0], cl_im[0], False)
    nbw = state_rows(cl_re[1], cl_im[1], True)

    s_i = jnp.arange(n)[:, None]
    t_i = jnp.arange(n)[None, :]

    def toeplitz(k, lower):
        k = k.reshape(g, S5_GROUP, n, S5_GROUP)
        tau = (t_i - s_i) if lower else (s_i - t_i)
        m = jnp.take(k, jnp.clip(tau, 0, n - 1).reshape(-1), axis=2).reshape(g, S5_GROUP, n, n, S5_GROUP)
        m = jnp.where((tau >= 0)[None, None, :, :, None], m, 0.0)
        return m.transpose(0, 2, 4, 3, 1).reshape(g, w, w)

    mf = toeplitz(kt[0], True)
    mb = toeplitz(kt[1], False)
    l16 = jnp.stack([l16r[0, :, :, 0], l16i[0, :, :, 0], l16r[1, :, :, 0], l16i[1, :, :, 0]], axis=1)
    l16 = jnp.pad(l16, [(0, 0), (0, 0), (0, hpad)])
    return wp, mf, mb, nf, nbw, l16


def kernel(x, c, ctx, c_ctx, mod_w, mod_b, norm_mix_g, w_in, shift_mu, rwkv_w0, rwkv_w_up, rwkv_a0, rwkv_a_up,
           rwkv_g_up, rwkv_k_k, rwkv_k_a, rwkv_r_k, lnx_w, lnx_b, s5_a_re, s5_a_im, s5_log_step, s5_b_re, s5_b_im,
           s5_c_re, s5_c_im, s5_d, s5_glu_w, s5_glu_b, w_out, norm_ffn_g, ffn_w_up, ffn_conv_w, ffn_conv_b,
           ffn_w_down, final_norm_g):
    assert mod_w.shape[0] == 1, "single trunk layer"
    b, t, d = x.shape
    ctx_len = ctx.shape[1]
    dr = rwkv_w0.shape[-1]
    ds = s5_d.shape[-1]
    dl, al, gl = rwkv_w_up.shape[2], rwkv_a_up.shape[2], rwkv_g_up.shape[1]
    n_groups = ds // S5_GROUP
    assert ctx_len == TOK_TILE and t % FFN_TOK_TILE == 0
    assert dr % MXU == 0 and s5_a_re.shape[-1] == S5_STATE and S5_CHUNK * S5_GROUP == MXU
    glp, wlp, alp = _round_up(gl, LANE), _round_up(2 * dl, LANE), _round_up(2 * al, LANE)
    dims = (dr, glp, wlp, alp)
    tc = ctx_len + t

    rows = _round_up(b + 1, 8)
    c_all = jnp.zeros((rows, d), F32).at[:b].set(c).at[b].set(c_ctx)
    mod3 = _mod(c_all, mod_w[0], mod_b).reshape(rows, 6, d)

    xc = jnp.concatenate([ctx, x], axis=1)
    w_packed = _pack_in(w_in[0], dr, (gl, dl, al), (glp, wlp, alp)).astype(BF16)
    mu_p = _pack_in(shift_mu[0], dr, (gl, dl, al), (glp, wlp, alp))
    p = _inproj(xc, mod3, norm_mix_g, w_packed, b)
    r, v, kk, lw, kd, bb, bonus, gate = _prep(
        p, mu_p, rwkv_w0[0], rwkv_a0[0], rwkv_k_k, rwkv_k_a, rwkv_r_k[0].reshape(1, dr),
        _lora_up(rwkv_w_up[0], wlp).astype(BF16), _lora_up(rwkv_a_up[0], alp).astype(BF16),
        jnp.pad(rwkv_g_up[0], [(0, glp - gl), (0, 0)]).astype(BF16), dims)

    y = _wkv(r, v, kk, lw, kd, bb, ctx_len // WKV_CHUNK)

    w = S5_CHUNK * S5_GROUP
    tile_col = lambda a: jnp.broadcast_to(a[..., None], a.shape + (w,))
    ls_t = jnp.broadcast_to(s5_log_step[0][:, :, None, None], (2, n_groups, S5_STATE, w))
    b_t = lambda a: jnp.tile(a, (1, 1, 1, S5_CHUNK))
    c_t = lambda a: jnp.tile(a.transpose(0, 1, 3, 2), (1, 1, 1, S5_CHUNK))
    tabs = _s5w(tile_col(s5_a_re[0]), tile_col(s5_a_im[0]), ls_t, b_t(s5_b_re[0]), b_t(s5_b_im[0]),
                c_t(s5_c_re[0]), c_t(s5_c_im[0]), s5_c_re[0], s5_c_im[0])
    wp, mf, mb, nf, nbw, l16 = _s5_tables(*tabs)
    nck = tc // S5_CHUNK
    u = p[..., p.shape[-1] - ds:]
    u_t = u.reshape(b, nck, S5_CHUNK, n_groups, S5_GROUP).transpose(3, 0, 1, 2, 4).reshape(n_groups, b * nck, w)
    d_t = jnp.tile(s5_d[0].reshape(n_groups, 1, S5_GROUP), (1, 1, S5_CHUNK))
    y5_t = _s5(u_t, wp, mf, mb, nf, nbw, l16, d_t, b, ctx_len // S5_CHUNK)
    y5 = y5_t.reshape(n_groups, b, nck, S5_CHUNK, S5_GROUP).transpose(1, 2, 3, 0, 4).reshape(b, tc, ds)

    x1, h2 = _merge(y, bonus, gate, y5, x, mod3, lnx_w, lnx_b, s5_glu_w[0].astype(BF16), s5_glu_b,
                    w_out[0].astype(BF16), norm_ffn_g, ctx_len // TOK_TILE)
    return _ffn(h2, ffn_w_up[0].astype(BF16), ffn_conv_w[0], ffn_conv_b, ffn_w_down[0].astype(BF16), x1, mod3,
                final_norm_g.reshape(1, d))
```

```python
import functools
import math

import jax
import jax.numpy as jnp
from jax import lax
from jax.experimental import pallas as pl
from jax.experimental.pallas import tpu as pltpu

F32 = jnp.float32
BF16 = jnp.bfloat16

HEAD = 64
GRID_W = 64
S5_GROUP = 16
S5_STATE = 64
RMS_EPS = 1e-6
LNX_EPS = 64e-5
LANE = 128
MXU = 256
HEADS_PER_GROUP = MXU // HEAD
WKV_CHUNK = 64
S5_CHUNK = 16
TOK_TILE = 256
FFN_TOK_TILE = 512
FFN_F_TILE = 512
VMEM_LIMIT = 56 * 1024 * 1024


def _cparams(sem):
    return pltpu.CompilerParams(dimension_semantics=sem, vmem_limit_bytes=VMEM_LIMIT)


def _round_up(n, m):
    return (n + m - 1) // m * m


def _split(a):
    hi = a.astype(BF16)
    lo = (a - hi.astype(F32)).astype(BF16)
    return hi, lo


def _mm(a, b, dims=(((1,), (0,)), ((), ())), split_a=False, split_b=False):
    dot = functools.partial(lax.dot_general, dimension_numbers=dims, preferred_element_type=F32)
    if not split_a and not split_b:
        return dot(a.astype(BF16), b.astype(BF16))
    a_hi, a_lo = _split(a) if split_a else (a.astype(BF16), None)
    b_hi, b_lo = _split(b) if split_b else (b.astype(BF16), None)
    out = dot(a_hi, b_hi)
    if a_lo is not None:
        out = out + dot(a_lo, b_hi)
    if b_lo is not None:
        out = out + dot(a_hi, b_lo)
    return out


_NT = (((1,), (1,)), ((), ()))
_TN = (((0,), (0,)), ((), ()))


def _sigmoid(x):
    return 1.0 / (1.0 + jnp.exp(-x))


def _gelu(x):
    return 0.5 * x * (1.0 + jnp.tanh(math.sqrt(2.0 / math.pi) * (x + 0.044715 * (x * x * x))))


def _head_ones():
    r = lax.broadcasted_iota(jnp.int32, (MXU, MXU), 0) // HEAD
    c = lax.broadcasted_iota(jnp.int32, (MXU, MXU), 1) // HEAD
    return jnp.where(r == c, 1.0, 0.0).astype(BF16)


def _head_sum(x, ones):
    parts = []
    for g in range(x.shape[1] // MXU):
        parts.append(_mm(x[:, g * MXU:(g + 1) * MXU], ones, split_a=True))
    return parts[0] if len(parts) == 1 else jnp.concatenate(parts, axis=1)


def _shift_rows(x, whole_tile=False):
    n = x.shape[0]
    row = lax.broadcasted_iota(jnp.int32, (n, 1), 0)
    pos = row & (GRID_W - 1)
    first = jnp.where(whole_tile, row, pos) == 0
    last = jnp.where(whole_tile, row - (n - 1), pos - (GRID_W - 1)) == 0
    prev = jnp.where(first, 0.0, pltpu.roll(x, 1, 0))
    nxt = jnp.where(last, 0.0, pltpu.roll(x, n - 1, 0))
    return prev, nxt


def _mod_kernel(c_ref, w_ref, b_ref, o_ref):
    c = c_ref[...]
    s = c * _sigmoid(c)
    o_ref[...] = lax.dot_general(s, w_ref[...], (((1,), (0,)), ((), ())), precision=lax.Precision.HIGHEST,
                                 preferred_element_type=F32) + b_ref[...]


def _mod(c_all, mod_w, mod_b):
    rows, d = c_all.shape
    n = mod_w.shape[1]
    tn = 1024
    return pl.pallas_call(
        _mod_kernel,
        grid=(n // tn,),
        in_specs=[pl.BlockSpec((rows, d), lambda j: (0, 0)),
                  pl.BlockSpec((d, tn), lambda j: (0, j)),
                  pl.BlockSpec((1, tn), lambda j: (0, j))],
        out_specs=pl.BlockSpec((rows, tn), lambda j: (0, j)),
        out_shape=jax.ShapeDtypeStruct((rows, n), F32),
        compiler_params=_cparams(("parallel",)),
        name="mod",
    )(c_all, mod_w, mod_b)


def _inproj_kernel(x_ref, mod_ref, g_ref, w_ref, o_ref):
    x = x_ref[...]
    ms = jnp.mean(x * x, axis=-1, keepdims=True)
    h = x * lax.rsqrt(ms + RMS_EPS) * g_ref[...]
    h = h * (1.0 + mod_ref[1:2, :]) + mod_ref[0:1, :]
    o_ref[...] = jnp.dot(h.astype(BF16), w_ref[...], preferred_element_type=F32)


def _inproj(xc, mod3, g, w_packed, n_batch):
    b, tc, d = xc.shape
    npk = w_packed.shape[1]
    nt = tc // TOK_TILE
    return pl.pallas_call(
        _inproj_kernel,
        grid=(b, nt),
        in_specs=[pl.BlockSpec((None, TOK_TILE, d), lambda bi, i: (bi, i, 0)),
                  pl.BlockSpec((None, 6, d), lambda bi, i: (jnp.where(i == 0, n_batch, bi), 0, 0)),
                  pl.BlockSpec((1, d), lambda bi, i: (0, 0)),
                  pl.BlockSpec((d, npk), lambda bi, i: (0, 0), pipeline_mode=pl.Buffered(1))],
        out_specs=pl.BlockSpec((None, TOK_TILE, npk), lambda bi, i: (bi, i, 0)),
        out_shape=jax.ShapeDtypeStruct((b, tc, npk), F32),
        compiler_params=_cparams(("parallel", "parallel")),
        name="inproj",
    )(xc, mod3, g, w_packed)


def _prep_kernel(dims, p_ref, mu_ref, w0_ref, a0_ref, kk_w_ref, ka_ref, rk_ref, wup_ref, aup_ref, gup_ref,
                 r_ref, v_ref, kk_ref, lw_ref, kd_ref, bb_ref, bonus_ref, g_ref):
    dr, glp, wlp, alp = dims
    nq = 3 * dr + glp + wlp + alp
    q = p_ref[:, :nq]
    prev, nxt = _shift_rows(q, whole_tile=pl.program_id(1) == 0)
    q = q + mu_ref[0:1, :] * (prev - q) + mu_ref[1:2, :] * (nxt - q)
    r = q[:, 0:dr]
    k = q[:, dr:2 * dr]
    v = q[:, 2 * dr:3 * dr]
    o = 3 * dr
    gd = q[:, o:o + glp]
    wd = q[:, o + glp:o + glp + wlp]
    ad = q[:, o + glp + wlp:o + glp + wlp + alp]

    ones = _head_ones()
    kraw = k * kk_w_ref[...]
    kk = kraw / jnp.maximum(jnp.sqrt(_head_sum(kraw * kraw, ones)), 1e-12)
    lora_w = _mm(jnp.tanh(wd), wup_ref[...], split_a=True)
    lora_a = _mm(ad, aup_ref[...], split_a=True)
    r_ref[...] = r
    v_ref[...] = v
    kk_ref[...] = kk
    for d in range(2):
        z = w0_ref[d:d + 1, :] + lora_w[:, d * dr:(d + 1) * dr]
        nz = -z
        softplus = jnp.maximum(nz, 0.0) + jnp.log1p(jnp.exp(-jnp.abs(nz)))
        w_log = -softplus - 0.5
        lw_ref[d] = -jnp.exp(w_log)
        a = _sigmoid(a0_ref[d:d + 1, :] + lora_a[:, d * dr:(d + 1) * dr])
        kd_ref[d] = k * (1.0 + (a - 1.0) * ka_ref[...])
        bb_ref[d] = kk * a
    bonus_ref[...] = _head_sum(r * k * rk_ref[...], ones) * v
    g_ref[...] = _mm(_sigmoid(gd), gup_ref[...])


def _prep(p, mu_p, w0, a0, k_k, k_a, r_k, wup_c, aup_c, gup_p, dims):
    b, tc, npk = p.shape
    dr, glp, wlp, alp = dims
    nq = 3 * dr + glp + wlp + alp
    nt = tc // TOK_TILE
    tok = lambda bi, i: (bi, i, 0)
    tok2 = lambda bi, i: (0, bi, i, 0)
    const = lambda bi, i: (0, 0)
    one = jax.ShapeDtypeStruct((b, tc, dr), F32)
    two = jax.ShapeDtypeStruct((2, b, tc, dr), F32)
    spec1 = pl.BlockSpec((None, TOK_TILE, dr), tok)
    spec2 = pl.BlockSpec((2, None, TOK_TILE, dr), tok2)
    return pl.pallas_call(
        functools.partial(_prep_kernel, dims),
        grid=(b, nt),
        in_specs=[pl.BlockSpec((None, TOK_TILE, npk), tok),
                  pl.BlockSpec((2, nq), const),
                  pl.BlockSpec((2, dr), const), pl.BlockSpec((2, dr), const),
                  pl.BlockSpec((1, dr), const), pl.BlockSpec((1, dr), const), pl.BlockSpec((1, dr), const),
                  pl.BlockSpec((wlp, 2 * dr), const), pl.BlockSpec((alp, 2 * dr), const),
                  pl.BlockSpec((glp, dr), const)],
        out_specs=[spec1, spec1, spec1, spec2, spec2, spec2, spec1, spec1],
        out_shape=[one, one, one, two, two, two, one, one],
        compiler_params=_cparams(("parallel", "parallel")),
        name="prep",
    )(p, mu_p, w0, a0, k_k, k_a, r_k, wup_c, aup_c, gup_p)


def _bd(x, bdmask):
    return jnp.where(bdmask, jnp.concatenate([x] * HEADS_PER_GROUP, axis=0), 0.0)


def _wkv_kernel(n_ctx_chunks, r_ref, v_ref, kk_ref, lw_ref, kd_ref, bb_ref, y_ref, s_ref):
    d = pl.program_id(1)
    i = pl.program_id(2)
    L = WKV_CHUNK
    n_groups = r_ref.shape[1] // MXU

    @pl.when(i == 0)
    def _():
        s_ref[...] = jnp.zeros_like(s_ref)

    fwd = d == 0
    sgn = 1 - 2 * d
    lw = lw_ref[...]
    row = lax.broadcasted_iota(jnp.int32, (L, 1), 0)
    pre = lw
    shift = 1
    while shift < L:
        pre = pre + jnp.where(row >= shift, pltpu.roll(pre, shift, 0), 0.0)
        shift *= 2
    cum = jnp.where(fwd, pre, pre[L - 1:L, :] - pre + lw)

    t_idx = lax.broadcasted_iota(jnp.int32, (L, MXU), 0)
    s_idx = lax.broadcasted_iota(jnp.int32, (L, MXU), 1) % HEAD
    strict = (t_idx - s_idx) * sgn > 0
    incl = (t_idx - s_idx) * sgn >= 0
    eye_sbs = jnp.where(t_idx == s_idx, 1.0, 0.0)
    br = lax.broadcasted_iota(jnp.int32, (MXU, MXU), 0)
    bc = lax.broadcasted_iota(jnp.int32, (MXU, MXU), 1)
    bdmask = (br // HEAD) == (bc // HEAD)
    eye_bd = jnp.where(br == bc, 1.0, 0.0)

    G = range(n_groups)
    sls = [slice(g * MXU, (g + 1) * MXU) for g in G]
    kt, bt, kdt, rt, vv, p_end = [], [], [], [], [], []
    for sl in sls:
        c = cum[:, sl]
        p_inv = jnp.exp(-c)
        kt.append(kk_ref[:, sl] * jnp.exp(c - lw[:, sl]))
        bt.append(bb_ref[:, sl] * p_inv)
        kdt.append(kd_ref[:, sl] * p_inv)
        rt.append(r_ref[:, sl] * jnp.exp(c))
        vv.append(v_ref[:, sl])
        p_end.append(jnp.exp(jnp.where(fwd, c[L - 1:L, :], c[0:1, :])))

    a_all = [_mm(jnp.concatenate([kt[g], rt[g]], axis=0),
                 jnp.concatenate([_bd(bt[g], bdmask), _bd(kdt[g], bdmask)], axis=0), _NT) for g in G]
    a_ab = [jnp.where(strict, a[:L, :MXU], 0.0) for a in a_all]
    a_ak = [jnp.where(strict, a[:L, MXU:], 0.0) for a in a_all]
    a_rb = [jnp.where(incl, a[L:, :MXU], 0.0) for a in a_all]
    a_rk = [jnp.where(incl, a[L:, MXU:], 0.0) for a in a_all]
    av = [_mm(jnp.concatenate([a_ak[g], a_rk[g]], axis=0), _bd(vv[g], bdmask)) for g in G]

    npow = [_mm(n, _bd(n, bdmask)) for n in a_ab]
    tinv = [eye_sbs - a for a in a_ab]
    for _ in range(int(math.log2(L)) - 2):
        both = [_mm(jnp.concatenate([n, t], axis=0), _bd(n, bdmask)) for n, t in zip(npow, tinv)]
        npow = [x[:L] for x in both]
        tinv = [t + x[L:] for t, x in zip(tinv, both)]
    tinv = [t + _mm(t, _bd(n, bdmask)) for t, n in zip(tinv, npow)]

    wu = [_mm(tinv[g], jnp.concatenate([_bd(kt[g], bdmask), _bd(av[g][:L], bdmask)], axis=1)) for g in G]
    wm = [w[:, :MXU] for w in wu]
    um = [w[:, MXU:] for w in wu]
    qy = [_mm(a_rb[g], jnp.concatenate([_bd(wm[g], bdmask), _bd(um[g], bdmask)], axis=1)) for g in G]
    btp = [bt[g] * p_end[g] for g in G]
    gmat = [jnp.where(bdmask, eye_bd * p_end[g] - _mm(btp[g], wm[g], _TN), 0.0) for g in G]
    hmat = [jnp.where(bdmask, _mm(jnp.concatenate([kdt[g] * p_end[g], btp[g]], axis=0),
                                  jnp.concatenate([vv[g], -um[g]], axis=0), _TN), 0.0) for g in G]
    ys = [_mm(jnp.concatenate([rt[g] - qy[g][:, :MXU], gmat[g]], axis=0), s_ref[g], split_b=True) for g in G]
    for g in G:
        y_ref[:, sls[g]] = ys[g][:L] + (av[g][L:] - qy[g][:, MXU:])
        s_ref[g] = ys[g][L:] + hmat[g]


def _wkv(r, v, kk, lw, kd, bb, n_ctx_chunks):
    b, tc, dr = r.shape
    nch = tc // WKV_CHUNK

    def chunk(d, i):
        bwd = jnp.where(i < n_ctx_chunks, n_ctx_chunks - 1 - i, nch + n_ctx_chunks - 1 - i)
        return jnp.where(d == 0, i, bwd)

    one = pl.BlockSpec((None, WKV_CHUNK, dr), lambda bi, d, i: (bi, chunk(d, i), 0))
    two = pl.BlockSpec((None, None, WKV_CHUNK, dr), lambda bi, d, i: (d, bi, chunk(d, i), 0))
    return pl.pallas_call(
        functools.partial(_wkv_kernel, n_ctx_chunks),
        grid=(b, 2, nch),
        in_specs=[one, one, one, two, two, two],
        out_specs=two,
        out_shape=jax.ShapeDtypeStruct((2, b, tc, dr), F32),
        scratch_shapes=[pltpu.VMEM((dr // MXU, MXU, MXU), F32)],
        compiler_params=_cparams(("parallel", "parallel", "arbitrary")),
        name="wkv",
    )(r, v, kk, lw, kd, bb)


def _s5w_kernel(ar_ref, ai_ref, ls_ref, br_ref, bi_ref, ctr_ref, cti_ref, cr_ref, ci_ref,
                lbr_ref, lbi_ref, clr_ref, cli_ref, kt_ref, l16r_ref, l16i_ref):
    ar, ai = ar_ref[...], ai_ref[...]
    dt = jnp.exp(ls_ref[...])
    mag = jnp.exp(ar * dt)
    ang = ai * dt
    lr, li = mag * jnp.cos(ang), mag * jnp.sin(ang)
    den = ar * ar + ai * ai
    nr = lr - 1.0
    f_re = (nr * ar + li * ai) / den
    f_im = (li * ar - nr * ai) / den
    b_re, b_im = br_ref[...], bi_ref[...]
    bb_re = f_re * b_re - f_im * b_im
    bb_im = f_re * b_im + f_im * b_re
    n = lax.broadcasted_iota(jnp.int32, ar.shape, ar.ndim - 1) // S5_GROUP
    pr, pi = jnp.ones_like(ar), jnp.zeros_like(ar)
    sr, si = lr, li
    for bit in range(int(math.log2(S5_CHUNK))):
        take = ((n >> bit) & 1) == 1
        pr, pi = jnp.where(take, pr * sr - pi * si, pr), jnp.where(take, pr * si + pi * sr, pi)
        sr, si = sr * sr - si * si, 2.0 * sr * si
    l16r_ref[...] = sr
    l16i_ref[...] = si
    lb_re = pr * bb_re - pi * bb_im
    lb_im = pr * bb_im + pi * bb_re
    lbr_ref[...] = lb_re
    lbi_ref[...] = lb_im
    c_re, c_im = ctr_ref[...], cti_ref[...]
    c1r = c_re * lr - c_im * li
    c1i = c_re * li + c_im * lr
    clr_ref[...] = c1r * pr - c1i * pi
    cli_ref[...] = -(c1r * pi + c1i * pr)
    hp = lax.Precision.HIGHEST
    kt_ref[...] = (jnp.einsum('ghp,gpn->ghn', cr_ref[...], lb_re, precision=hp, preferred_element_type=F32)
                   - jnp.einsum('ghp,gpn->ghn', ci_ref[...], lb_im, precision=hp, preferred_element_type=F32))


def _s5w(ar_t, ai_t, ls_t, br_t, bi_t, ctr_t, cti_t, c_re, c_im):
    _, g, p, w = ar_t.shape
    gb = 8
    big = pl.BlockSpec((None, gb, p, w), lambda d, j: (d, j, 0, 0))
    cs = pl.BlockSpec((None, gb, S5_GROUP, p), lambda d, j: (d, j, 0, 0))
    ks = pl.BlockSpec((None, gb, S5_GROUP, w), lambda d, j: (d, j, 0, 0))
    big_shape = jax.ShapeDtypeStruct((2, g, p, w), F32)
    return pl.pallas_call(
        _s5w_kernel,
        grid=(2, g // gb),
        in_specs=[big] * 7 + [cs, cs],
        out_specs=[big, big, big, big, ks, big, big],
        out_shape=[big_shape] * 4 + [jax.ShapeDtypeStruct((2, g, S5_GROUP, w), F32)] + [big_shape] * 2,
        compiler_params=_cparams(("parallel", "parallel")),
        name="s5w",
    )(ar_t, ai_t, ls_t, br_t, bi_t, ctr_t, cti_t, c_re, c_im)


def _s5_kernel(n_ctx_chunks, nb, u_ref, wp_ref, mf_ref, mb_ref, nf_ref, nb_ref, l16_ref, d_ref, y_ref,
               e_ref, sf_ref, sb_ref):
    gb = u_ref.shape[0]
    lanes = gb * nb
    nck = e_ref.shape[1] // lanes
    for g in range(gb):
        e = _mm(u_ref[g], wp_ref[g], split_a=True, split_b=True)
        for b in range(nb):
            for k in range(4):
                e_ref[k, pl.ds(g * nb + b, nck, stride=lanes), :] = e[b * nck:(b + 1) * nck, k * LANE:(k + 1) * LANE]

    def lam_rows(k):
        return jnp.concatenate([jnp.broadcast_to(l16_ref[g, k:k + 1, :], (nb, LANE)) for g in range(gb)], axis=0)

    lfr, lfi, lbr, lbi = (lam_rows(k) for k in range(4))
    zero = jnp.zeros((lanes, LANE), F32)

    def step(s_ref, lr, li, slot0, c, carry):
        sr, si = carry
        base = pl.multiple_of(c * lanes, lanes)
        s_ref[0, pl.ds(base, lanes), :] = sr
        s_ref[1, pl.ds(base, lanes), :] = si
        er = e_ref[slot0, pl.ds(base, lanes), :]
        ei = e_ref[slot0 + 1, pl.ds(base, lanes), :]
        return (lr * sr - li * si + er, lr * si + li * sr + ei)

    fwd_step = functools.partial(step, sf_ref, lfr, lfi, 0)
    bwd_step = functools.partial(step, sb_ref, lbr, lbi, 2)
    lax.fori_loop(0, nck, fwd_step, (zero, zero))
    carry = lax.fori_loop(0, n_ctx_chunks, lambda k, cy: bwd_step(n_ctx_chunks - 1 - k, cy), (zero, zero))
    lax.fori_loop(0, nck - n_ctx_chunks, lambda k, cy: bwd_step(nck - 1 - k, cy), carry)

    def rows_of(s_ref, g):
        return jnp.concatenate(
            [jnp.concatenate([s_ref[k, pl.ds(g * nb + b, nck, stride=lanes), :] for k in range(2)], axis=1)
             for b in range(nb)], axis=0)

    for g in range(gb):
        u = u_ref[g]
        y = _mm(u, mf_ref[g] + mb_ref[g], split_a=True, split_b=True)
        y = y + _mm(rows_of(sf_ref, g), nf_ref[g], split_a=True, split_b=True)
        y = y + _mm(rows_of(sb_ref, g), nb_ref[g], split_a=True, split_b=True)
        y_ref[g] = y + u * d_ref[g]


def _s5(u_t, wp, mf, mb, nf, nbw, l16, d_t, n_batch, n_ctx_chunks):
    g, rows, w = u_t.shape
    nck = rows // n_batch
    gb = 4
    grp = lambda shape: pl.BlockSpec((gb,) + shape, lambda j: (j,) + (0,) * len(shape))
    return pl.pallas_call(
        functools.partial(_s5_kernel, n_ctx_chunks, n_batch),
        grid=(g // gb,),
        in_specs=[grp((rows, w)), grp((w, 4 * LANE)), grp((w, w)), grp((w, w)),
                  grp((2 * LANE, w)), grp((2 * LANE, w)), grp((4, LANE)), grp((1, w))],
        out_specs=grp((rows, w)),
        out_shape=jax.ShapeDtypeStruct((g, rows, w), F32),
        scratch_shapes=[pltpu.VMEM((4, nck * gb * n_batch, LANE), F32),
                        pltpu.VMEM((2, nck * gb * n_batch, LANE), F32),
                        pltpu.VMEM((2, nck * gb * n_batch, LANE), F32)],
        compiler_params=_cparams(("parallel",)),
        name="s5",
    )(u_t, wp, mf, mb, nf, nbw, l16, d_t)


def _merge_kernel(yf_ref, yb_ref, bonus_ref, g_ref, y5_ref, x_ref, mod_ref, lnw_ref, lnb_ref, gluw_ref, glub_ref,
                  wout_ref, gffn_ref, x1_ref, h2_ref):
    ones = _head_ones()
    yh = yf_ref[...] + yb_ref[...]
    mean = _head_sum(yh, ones) * (1.0 / HEAD)
    cen = yh - mean
    var = _head_sum(cen * cen, ones) * (1.0 / HEAD)
    yn = cen * lax.rsqrt(var + LNX_EPS) * lnw_ref[...] + lnb_ref[...]
    o_rwkv = (yn + bonus_ref[...]) * g_ref[...]
    z = _gelu(y5_ref[...])
    o_s5 = z * _sigmoid(_mm(z, gluw_ref[...]) + glub_ref[...])
    o = jnp.concatenate([o_rwkv, o_s5], axis=1)
    mix = _mm(o, wout_ref[...])
    x1 = x_ref[...] + mod_ref[2:3, :] * mix
    x1_ref[...] = x1
    ms = jnp.mean(x1 * x1, axis=-1, keepdims=True)
    h = x1 * lax.rsqrt(ms + RMS_EPS) * gffn_ref[...]
    h2_ref[...] = (h * (1.0 + mod_ref[4:5, :]) + mod_ref[3:4, :]).astype(BF16)


def _merge(y, bonus, g, y5, x, mod3, lnx_w, lnx_b, glu_w, glu_b, w_out, g_ffn, n_ctx_tiles):
    b, t, d = x.shape
    dr = bonus.shape[-1]
    ds = y5.shape[-1]
    nt = t // TOK_TILE
    off = lambda bi, i: (bi, i + n_ctx_tiles, 0)
    const = lambda bi, i: (0, 0)
    return pl.pallas_call(
        _merge_kernel,
        grid=(b, nt),
        in_specs=[pl.BlockSpec((None, None, TOK_TILE, dr), lambda bi, i: (0, bi, i + n_ctx_tiles, 0)),
                  pl.BlockSpec((None, None, TOK_TILE, dr), lambda bi, i: (1, bi, i + n_ctx_tiles, 0)),
                  pl.BlockSpec((None, TOK_TILE, dr), off),
                  pl.BlockSpec((None, TOK_TILE, dr), off),
                  pl.BlockSpec((None, TOK_TILE, ds), off),
                  pl.BlockSpec((None, TOK_TILE, d), lambda bi, i: (bi, i, 0)),
                  pl.BlockSpec((None, 6, d), lambda bi, i: (bi, 0, 0)),
                  pl.BlockSpec((1, dr), const), pl.BlockSpec((1, dr), const),
                  pl.BlockSpec((ds, ds), const), pl.BlockSpec((1, ds), const),
                  pl.BlockSpec((dr + ds, d), const), pl.BlockSpec((1, d), const)],
        out_specs=[pl.BlockSpec((None, TOK_TILE, d), lambda bi, i: (bi, i, 0)),
                   pl.BlockSpec((None, TOK_TILE, d), lambda bi, i: (bi, i, 0))],
        out_shape=[jax.ShapeDtypeStruct((b, t, d), F32), jax.ShapeDtypeStruct((b, t, d), BF16)],
        compiler_params=_cparams(("parallel", "parallel")),
        name="merge",
    )(y, y, bonus, g, y5, x, mod3, lnx_w, lnx_b, glu_w, glu_b, w_out, g_ffn)


def _ffn_kernel(h_ref, wg_ref, wv_ref, cw_ref, cb_ref, wd_ref, x1_ref, mod_ref, gfin_ref, o_ref, acc_ref):
    f = pl.program_id(1)

    @pl.when(f == 0)
    def _():
        acc_ref[...] = jnp.zeros_like(acc_ref)

    h = h_ref[...]
    gate = jnp.dot(h, wg_ref[...], preferred_element_type=F32)
    val = jnp.dot(h, wv_ref[...], preferred_element_type=F32)
    prev, nxt = _shift_rows(gate)
    gate = cw_ref[0:1, :] * prev + cw_ref[1:2, :] * gate + cw_ref[2:3, :] * nxt + cb_ref[...]
    act = (_gelu(gate) * val).astype(BF16)
    acc_ref[...] += jnp.dot(act, wd_ref[...], preferred_element_type=F32)

    @pl.when(f == pl.num_programs(1) - 1)
    def _():
        x2 = x1_ref[...] + mod_ref[5:6, :] * acc_ref[...]
        ms = jnp.mean(x2 * x2, axis=-1, keepdims=True)
        o_ref[...] = x2 * lax.rsqrt(ms + RMS_EPS) * gfin_ref[...]


def _ffn(h2, w_up, conv_w, conv_b, w_down, x1, mod3, g_fin):
    b, t, d = x1.shape
    ff = w_down.shape[0]
    tm = min(FFN_TOK_TILE, t)
    tf = min(FFN_F_TILE, ff)
    nt, nf = t // tm, ff // tf
    return pl.pallas_call(
        _ffn_kernel,
        grid=(b * nt, nf),
        in_specs=[pl.BlockSpec((None, tm, d), lambda i, f: (i // nt, i % nt, 0)),
                  pl.BlockSpec((d, tf), lambda i, f: (0, f)),
                  pl.BlockSpec((d, tf), lambda i, f: (0, nf + f)),
                  pl.BlockSpec((3, tf), lambda i, f: (0, f)),
                  pl.BlockSpec((1, tf), lambda i, f: (0, f)),
                  pl.BlockSpec((tf, d), lambda i, f: (f, 0)),
                  pl.BlockSpec((None, tm, d), lambda i, f: (i // nt, i % nt, 0)),
                  pl.BlockSpec((None, 6, d), lambda i, f: (i // nt, 0, 0)),
                  pl.BlockSpec((1, d), lambda i, f: (0, 0))],
        out_specs=pl.BlockSpec((None, tm, d), lambda i, f: (i // nt, i % nt, 0)),
        out_shape=jax.ShapeDtypeStruct((b, t, d), F32),
        scratch_shapes=[pltpu.VMEM((tm, d), F32)],
        compiler_params=_cparams(("parallel", "arbitrary")),
        name="ffn",
    )(h2, w_up, w_up, conv_w, conv_b, w_down, x1, mod3, g_fin)


def _pad_cols(a, n):
    return jnp.pad(a, [(0, 0)] * (a.ndim - 1) + [(0, n - a.shape[-1])])


def _pack_in(a, dr, sizes, pads):
    gl, dl, al = sizes
    glp, wlp, alp = pads
    o = 3 * dr
    parts = [a[..., :o], _pad_cols(a[..., o:o + gl], glp), _pad_cols(a[..., o + gl:o + gl + 2 * dl], wlp),
             _pad_cols(a[..., o + gl + 2 * dl:o + gl + 2 * dl + 2 * al], alp), a[..., o + gl + 2 * dl + 2 * al:]]
    return jnp.concatenate(parts, axis=-1)


def _lora_up(w, rows):
    _, r, dr = w.shape
    out = jnp.zeros((rows, 2 * dr), w.dtype)
    out = out.at[0:r, 0:dr].set(w[0])
    out = out.at[r:2 * r, dr:2 * dr].set(w[1])
    return out


def _s5_tables(lb_re, lb_im, cl_re, cl_im, kt, l16r, l16i):
    _, g, p, w = lb_re.shape
    n = S5_CHUNK
    hpad = LANE - p

    def rows_sj(a, flip):
        a = a.reshape(g, p, n, S5_GROUP)
        if flip:
            a = a[:, :, ::-1, :]
        return a.transpose(0, 2, 3, 1).reshape(g, w, p)

    def padp(a):
        return jnp.pad(a, [(0, 0), (0, 0), (0, hpad)])

    wp = jnp.concatenate([padp(rows_sj(lb_re[0], True)), padp(rows_sj(lb_im[0], True)),
                          padp(rows_sj(lb_re[1], False)), padp(rows_sj(lb_im[1], False))], axis=-1)

    def state_rows(re, im, flip):
        def one(a):
            a = a.reshape(g, p, n, S5_GROUP)
            if flip:
                a = a[:, :, ::-1, :]
            return jnp.pad(a.reshape(g, p, w), [(0, 0), (0, hpad), (0, 0)])
        return jnp.concatenate([one(re), one(im)], axis=1)

    nf = state_rows(cl_re[0], cl_im[0], False)
    nbw = state_rows(cl_re[1], cl_im[1], True)

    s_i = jnp.arange(n)[:, None]
    t_i = jnp.arange(n)[None, :]

    def toeplitz(k, lower):
        k = k.reshape(g, S5_GROUP, n, S5_GROUP)
        tau = (t_i - s_i) if lower else (s_i - t_i)
        m = jnp.take(k, jnp.clip(tau, 0, n - 1).reshape(-1), axis=2).reshape(g, S5_GROUP, n, n, S5_GROUP)
        m = jnp.where((tau >= 0)[None, None, :, :, None], m, 0.0)
        return m.transpose(0, 2, 4, 3, 1).reshape(g, w, w)

    mf = toeplitz(kt[0], True)
    mb = toeplitz(kt[1], False)
    l16 = jnp.stack([l16r[0, :, :, 0], l16i[0, :, :, 0], l16r[1, :, :, 0], l16i[1, :, :, 0]], axis=1)
    l16 = jnp.pad(l16, [(0, 0), (0, 0), (0, hpad)])
    return wp, mf, mb, nf, nbw, l16


def kernel(x, c, ctx, c_ctx, mod_w, mod_b, norm_mix_g, w_in, shift_mu, rwkv_w0, rwkv_w_up, rwkv_a0, rwkv_a_up,
           rwkv_g_up, rwkv_k_k, rwkv_k_a, rwkv_r_k, lnx_w, lnx_b, s5_a_re, s5_a_im, s5_log_step, s5_b_re, s5_b_im,
           s5_c_re, s5_c_im, s5_d, s5_glu_w, s5_glu_b, w_out, norm_ffn_g, ffn_w_up, ffn_conv_w, ffn_conv_b,
           ffn_w_down, final_norm_g):
    assert mod_w.shape[0] == 1, "single trunk layer"
    b, t, d = x.shape
    ctx_len = ctx.shape[1]
    dr = rwkv_w0.shape[-1]
    ds = s5_d.shape[-1]
    dl, al, gl = rwkv_w_up.shape[2], rwkv_a_up.shape[2], rwkv_g_up.shape[1]
    n_groups = ds // S5_GROUP
    assert ctx_len == TOK_TILE and t % FFN_TOK_TILE == 0
    assert dr % MXU == 0 and s5_a_re.shape[-1] == S5_STATE and S5_CHUNK * S5_GROUP == MXU
    glp, wlp, alp = _round_up(gl, LANE), _round_up(2 * dl, LANE), _round_up(2 * al, LANE)
    dims = (dr, glp, wlp, alp)
    tc = ctx_len + t

    rows = _round_up(b + 1, 8)
    c_all = jnp.zeros((rows, d), F32).at[:b].set(c).at[b].set(c_ctx)
    mod3 = _mod(c_all, mod_w[0], mod_b).reshape(rows, 6, d)

    xc = jnp.concatenate([ctx, x], axis=1)
    w_packed = _pack_in(w_in[0], dr, (gl, dl, al), (glp, wlp, alp)).astype(BF16)
    mu_p = _pack_in(shift_mu[0], dr, (gl, dl, al), (glp, wlp, alp))
    p = _inproj(xc, mod3, norm_mix_g, w_packed, b)
    r, v, kk, lw, kd, bb, bonus, gate = _prep(
        p, mu_p, rwkv_w0[0], rwkv_a0[0], rwkv_k_k, rwkv_k_a, rwkv_r_k[0].reshape(1, dr),
        _lora_up(rwkv_w_up[0], wlp).astype(BF16), _lora_up(rwkv_a_up[0], alp).astype(BF16),
        jnp.pad(rwkv_g_up[0], [(0, glp - gl), (0, 0)]).astype(BF16), dims)

    y = _wkv(r, v, kk, lw, kd, bb, ctx_len // WKV_CHUNK)

    w = S5_CHUNK * S5_GROUP
    tile_col = lambda a: jnp.broadcast_to(a[..., None], a.shape + (w,))
    ls_t = jnp.broadcast_to(s5_log_step[0][:, :, None, None], (2, n_groups, S5_STATE, w))
    b_t = lambda a: jnp.tile(a, (1, 1, 1, S5_CHUNK))
    c_t = lambda a: jnp.tile(a.transpose(0, 1, 3, 2), (1, 1, 1, S5_CHUNK))
    tabs = _s5w(tile_col(s5_a_re[0]), tile_col(s5_a_im[0]), ls_t, b_t(s5_b_re[0]), b_t(s5_b_im[0]),
                c_t(s5_c_re[0]), c_t(s5_c_im[0]), s5_c_re[0], s5_c_im[0])
    wp, mf, mb, nf, nbw, l16 = _s5_tables(*tabs)
    nck = tc // S5_CHUNK
    u = p[..., p.shape[-1] - ds:]
    u_t = u.reshape(b, nck, S5_CHUNK, n_groups, S5_GROUP).transpose(3, 0, 1, 2, 4).reshape(n_groups, b * nck, w)
    d_t = jnp.tile(s5_d[0].reshape(n_groups, 1, S5_GROUP), (1, 1, S5_CHUNK))
    y5_t = _s5(u_t, wp, mf, mb, nf, nbw, l16, d_t, b, ctx_len // S5_CHUNK)
    y5 = y5_t.reshape(n_groups, b, nck, S5_CHUNK, S5_GROUP).transpose(1, 2, 3, 0, 4).reshape(b, tc, ds)

    x1, h2 = _merge(y, bonus, gate, y5, x, mod3, lnx_w, lnx_b, s5_glu_w[0].astype(BF16), s5_glu_b,
                    w_out[0].astype(BF16), norm_ffn_g, ctx_len // TOK_TILE)
    return _ffn(h2, ffn_w_up[0].astype(BF16), ffn_conv_w[0], ffn_conv_b, ffn_w_down[0].astype(BF16), x1, mod3,
                final_norm_g.reshape(1, d))
```

```python
import functools
import math

import jax
import jax.numpy as jnp
from jax import lax
from jax.experimental import pallas as pl
from jax.experimental.pallas import tpu as pltpu

F32 = jnp.float32
BF16 = jnp.bfloat16

HEAD = 64
GRID_W = 64
S5_GROUP = 16
S5_STATE = 64
RMS_EPS = 1e-6
LNX_EPS = 64e-5
LANE = 128
MXU = 256
HEADS_PER_GROUP = MXU // HEAD
WKV_CHUNK = 64
S5_CHUNK = 16
TOK_TILE = 256
FFN_TOK_TILE = 512
FFN_F_TILE = 512
VMEM_LIMIT = 56 * 1024 * 1024


def _cparams(sem):
    return pltpu.CompilerParams(dimension_semantics=sem, vmem_limit_bytes=VMEM_LIMIT)


def _round_up(n, m):
    return (n + m - 1) // m * m


def _split(a):
    hi = a.astype(BF16)
    lo = (a - hi.astype(F32)).astype(BF16)
    return hi, lo


def _mm(a, b, dims=(((1,), (0,)), ((), ())), split_a=False, split_b=False):
    dot = functools.partial(lax.dot_general, dimension_numbers=dims, preferred_element_type=F32)
    if not split_a and not split_b:
        return dot(a.astype(BF16), b.astype(BF16))
    a_hi, a_lo = _split(a) if split_a else (a.astype(BF16), None)
    b_hi, b_lo = _split(b) if split_b else (b.astype(BF16), None)
    out = dot(a_hi, b_hi)
    if a_lo is not None:
        out = out + dot(a_lo, b_hi)
    if b_lo is not None:
        out = out + dot(a_hi, b_lo)
    return out


_NT = (((1,), (1,)), ((), ()))
_TN = (((0,), (0,)), ((), ()))


def _sigmoid(x):
    return 1.0 / (1.0 + jnp.exp(-x))


def _gelu(x):
    return 0.5 * x * (1.0 + jnp.tanh(math.sqrt(2.0 / math.pi) * (x + 0.044715 * (x * x * x))))


def _head_ones():
    r = lax.broadcasted_iota(jnp.int32, (MXU, MXU), 0) // HEAD
    c = lax.broadcasted_iota(jnp.int32, (MXU, MXU), 1) // HEAD
    return jnp.where(r == c, 1.0, 0.0).astype(BF16)


def _head_sum(x, ones):
    parts = []
    for g in range(x.shape[1] // MXU):
        parts.append(_mm(x[:, g * MXU:(g + 1) * MXU], ones, split_a=True))
    return parts[0] if len(parts) == 1 else jnp.concatenate(parts, axis=1)


def _shift_rows(x, whole_tile=False):
    n = x.shape[0]
    row = lax.broadcasted_iota(jnp.int32, (n, 1), 0)
    pos = row & (GRID_W - 1)
    first = jnp.where(whole_tile, row, pos) == 0
    last = jnp.where(whole_tile, row - (n - 1), pos - (GRID_W - 1)) == 0
    prev = jnp.where(first, 0.0, pltpu.roll(x, 1, 0))
    nxt = jnp.where(last, 0.0, pltpu.roll(x, n - 1, 0))
    return prev, nxt


def _mod_kernel(c_ref, w_ref, b_ref, o_ref):
    c = c_ref[...]
    s = c * _sigmoid(c)
    o_ref[...] = lax.dot_general(s, w_ref[...], (((1,), (0,)), ((), ())), precision=lax.Precision.HIGHEST,
                                 preferred_element_type=F32) + b_ref[...]


def _mod(c_all, mod_w, mod_b):
    rows, d = c_all.shape
    n = mod_w.shape[1]
    tn = 1024
    return pl.pallas_call(
        _mod_kernel,
        grid=(n // tn,),
        in_specs=[pl.BlockSpec((rows, d), lambda j: (0, 0)),
                  pl.BlockSpec((d, tn), lambda j: (0, j)),
                  pl.BlockSpec((1, tn), lambda j: (0, j))],
        out_specs=pl.BlockSpec((rows, tn), lambda j: (0, j)),
        out_shape=jax.ShapeDtypeStruct((rows, n), F32),
        compiler_params=_cparams(("parallel",)),
        name="mod",
    )(c_all, mod_w, mod_b)


def _inproj_kernel(c_ref, x_ref, mod_ref, g_ref, w_ref, o_ref):
    x = jnp.where(pl.program_id(1) == 0, c_ref[...], x_ref[...])
    ms = jnp.mean(x * x, axis=-1, keepdims=True)
    h = x * lax.rsqrt(ms + RMS_EPS) * g_ref[...]
    h = h * (1.0 + mod_ref[1:2, :]) + mod_ref[0:1, :]
    o_ref[...] = jnp.dot(h.astype(BF16), w_ref[...], preferred_element_type=F32)


def _inproj(ctx, x, mod3, g, w_packed, n_batch):
    b, t, d = x.shape
    npk = w_packed.shape[1]
    tc = ctx.shape[1] + t
    nt = tc // TOK_TILE
    return pl.pallas_call(
        _inproj_kernel,
        grid=(b, nt),
        in_specs=[pl.BlockSpec((None, TOK_TILE, d), lambda bi, i: (bi, 0, 0)),
                  pl.BlockSpec((None, TOK_TILE, d), lambda bi, i: (bi, jnp.maximum(i - 1, 0), 0)),
                  pl.BlockSpec((None, 6, d), lambda bi, i: (jnp.where(i == 0, n_batch, bi), 0, 0)),
                  pl.BlockSpec((1, d), lambda bi, i: (0, 0)),
                  pl.BlockSpec((d, npk), lambda bi, i: (0, 0), pipeline_mode=pl.Buffered(1))],
        out_specs=pl.BlockSpec((None, TOK_TILE, npk), lambda bi, i: (bi, i, 0)),
        out_shape=jax.ShapeDtypeStruct((b, tc, npk), F32),
        compiler_params=_cparams(("parallel", "parallel")),
        name="inproj",
    )(ctx, x, mod3, g, w_packed)


def _prep_kernel(dims, p_ref, mu_ref, w0_ref, a0_ref, kk_w_ref, ka_ref, rk_ref, wup_ref, aup_ref, gup_ref,
                 r_ref, v_ref, kk_ref, lw_ref, kd_ref, bb_ref, bonus_ref, g_ref):
    dr, glp, wlp, alp = dims
    nq = 3 * dr + glp + wlp + alp
    q = p_ref[:, :nq]
    prev, nxt = _shift_rows(q, whole_tile=pl.program_id(1) == 0)
    q = q + mu_ref[0:1, :] * (prev - q) + mu_ref[1:2, :] * (nxt - q)
    r = q[:, 0:dr]
    k = q[:, dr:2 * dr]
    v = q[:, 2 * dr:3 * dr]
    o = 3 * dr
    gd = q[:, o:o + glp]
    wd = q[:, o + glp:o + glp + wlp]
    ad = q[:, o + glp + wlp:o + glp + wlp + alp]

    ones = _head_ones()
    kraw = k * kk_w_ref[...]
    kk = kraw / jnp.maximum(jnp.sqrt(_head_sum(kraw * kraw, ones)), 1e-12)
    lora_w = _mm(jnp.tanh(wd), wup_ref[...], split_a=True)
    lora_a = _mm(ad, aup_ref[...], split_a=True)
    r_ref[...] = r.astype(r_ref.dtype)
    v_ref[...] = v.astype(v_ref.dtype)
    kk_ref[...] = kk.astype(kk_ref.dtype)
    for d in range(2):
        z = w0_ref[d:d + 1, :] + lora_w[:, d * dr:(d + 1) * dr]
        nz = -z
        softplus = jnp.maximum(nz, 0.0) + jnp.log1p(jnp.exp(-jnp.abs(nz)))
        w_log = -softplus - 0.5
        lw_ref[d] = -jnp.exp(w_log)
        a = _sigmoid(a0_ref[d:d + 1, :] + lora_a[:, d * dr:(d + 1) * dr])
        kd_ref[d] = (k * (1.0 + (a - 1.0) * ka_ref[...])).astype(kd_ref.dtype)
        bb_ref[d] = (kk * a).astype(bb_ref.dtype)
    bonus_ref[...] = (_head_sum(r * k * rk_ref[...], ones) * v).astype(bonus_ref.dtype)
    g_ref[...] = _mm(_sigmoid(gd), gup_ref[...]).astype(g_ref.dtype)


def _prep(p, mu_p, w0, a0, k_k, k_a, r_k, wup_c, aup_c, gup_p, dims):
    b, tc, npk = p.shape
    dr, glp, wlp, alp = dims
    nq = 3 * dr + glp + wlp + alp
    nt = tc // TOK_TILE
    tok = lambda bi, i: (bi, i, 0)
    tok2 = lambda bi, i: (0, bi, i, 0)
    const = lambda bi, i: (0, 0)
    one = jax.ShapeDtypeStruct((b, tc, dr), BF16)
    two = jax.ShapeDtypeStruct((2, b, tc, dr), BF16)
    lws = jax.ShapeDtypeStruct((2, b, tc, dr), F32)
    spec1 = pl.BlockSpec((None, TOK_TILE, dr), tok)
    spec2 = pl.BlockSpec((2, None, TOK_TILE, dr), tok2)
    return pl.pallas_call(
        functools.partial(_prep_kernel, dims),
        grid=(b, nt),
        in_specs=[pl.BlockSpec((None, TOK_TILE, npk), tok),
                  pl.BlockSpec((2, nq), const),
                  pl.BlockSpec((2, dr), const), pl.BlockSpec((2, dr), const),
                  pl.BlockSpec((1, dr), const), pl.BlockSpec((1, dr), const), pl.BlockSpec((1, dr), const),
                  pl.BlockSpec((wlp, 2 * dr), const), pl.BlockSpec((alp, 2 * dr), const),
                  pl.BlockSpec((glp, dr), const)],
        out_specs=[spec1, spec1, spec1, spec2, spec2, spec2, spec1, spec1],
        out_shape=[one, one, one, lws, two, two, one, one],
        compiler_params=_cparams(("parallel", "parallel")),
        name="prep",
    )(p, mu_p, w0, a0, k_k, k_a, r_k, wup_c, aup_c, gup_p)


def _bd(x, bdmask):
    return jnp.where(bdmask, jnp.concatenate([x] * HEADS_PER_GROUP, axis=0), 0.0)


def _wkv_kernel(n_ctx_chunks, r_ref, v_ref, kk_ref, lw_ref, kd_ref, bb_ref, y_ref, s_ref):
    d = pl.program_id(1)
    i = pl.program_id(2)
    L = WKV_CHUNK
    n_groups = r_ref.shape[1] // MXU

    @pl.when(i == 0)
    def _():
        s_ref[...] = jnp.zeros_like(s_ref)

    fwd = d == 0
    sgn = 1 - 2 * d
    lw = lw_ref[...]
    row = lax.broadcasted_iota(jnp.int32, (L, 1), 0)
    pre = lw
    shift = 1
    while shift < L:
        pre = pre + jnp.where(row >= shift, pltpu.roll(pre, shift, 0), 0.0)
        shift *= 2
    cum = jnp.where(fwd, pre, pre[L - 1:L, :] - pre + lw)

    t_idx = lax.broadcasted_iota(jnp.int32, (L, MXU), 0)
    s_idx = lax.broadcasted_iota(jnp.int32, (L, MXU), 1) % HEAD
    strict = (t_idx - s_idx) * sgn > 0
    incl = (t_idx - s_idx) * sgn >= 0
    eye_sbs = jnp.where(t_idx == s_idx, 1.0, 0.0)
    br = lax.broadcasted_iota(jnp.int32, (MXU, MXU), 0)
    bc = lax.broadcasted_iota(jnp.int32, (MXU, MXU), 1)
    bdmask = (br // HEAD) == (bc // HEAD)
    eye_bd = jnp.where(br == bc, 1.0, 0.0)

    G = range(n_groups)
    sls = [slice(g * MXU, (g + 1) * MXU) for g in G]
    kt, bt, kdt, rt, vv, p_end = [], [], [], [], [], []
    for sl in sls:
        c = cum[:, sl]
        p_inv = jnp.exp(-c)
        kt.append(kk_ref[:, sl].astype(F32) * jnp.exp(c - lw[:, sl]))
        bt.append(bb_ref[:, sl].astype(F32) * p_inv)
        kdt.append(kd_ref[:, sl].astype(F32) * p_inv)
        rt.append(r_ref[:, sl].astype(F32) * jnp.exp(c))
        vv.append(v_ref[:, sl].astype(F32))
        p_end.append(jnp.exp(jnp.where(fwd, c[L - 1:L, :], c[0:1, :])))

    a_all = [_mm(jnp.concatenate([kt[g], rt[g]], axis=0),
                 jnp.concatenate([_bd(bt[g], bdmask), _bd(kdt[g], bdmask)], axis=0), _NT) for g in G]
    a_ab = [jnp.where(strict, a[:L, :MXU], 0.0) for a in a_all]
    a_ak = [jnp.where(strict, a[:L, MXU:], 0.0) for a in a_all]
    a_rb = [jnp.where(incl, a[L:, :MXU], 0.0) for a in a_all]
    a_rk = [jnp.where(incl, a[L:, MXU:], 0.0) for a in a_all]
    av = [_mm(jnp.concatenate([a_ak[g], a_rk[g]], axis=0), _bd(vv[g], bdmask)) for g in G]

    npow = [_mm(n, _bd(n, bdmask)) for n in a_ab]
    tinv = [eye_sbs - a for a in a_ab]
    for _ in range(int(math.log2(L)) - 2):
        both = [_mm(jnp.concatenate([n, t], axis=0), _bd(n, bdmask)) for n, t in zip(npow, tinv)]
        npow = [x[:L] for x in both]
        tinv = [t + x[L:] for t, x in zip(tinv, both)]
    tinv = [t + _mm(t, _bd(n, bdmask)) for t, n in zip(tinv, npow)]

    wu = [_mm(tinv[g], jnp.concatenate([_bd(kt[g], bdmask), _bd(av[g][:L], bdmask)], axis=1)) for g in G]
    wm = [w[:, :MXU] for w in wu]
    um = [w[:, MXU:] for w in wu]
    qy = [_mm(a_rb[g], jnp.concatenate([_bd(wm[g], bdmask), _bd(um[g], bdmask)], axis=1)) for g in G]
    btp = [bt[g] * p_end[g] for g in G]
    gmat = [jnp.where(bdmask, eye_bd * p_end[g] - _mm(btp[g], wm[g], _TN), 0.0) for g in G]
    hmat = [jnp.where(bdmask, _mm(jnp.concatenate([kdt[g] * p_end[g], btp[g]], axis=0),
                                  jnp.concatenate([vv[g], -um[g]], axis=0), _TN), 0.0) for g in G]
    ys = [_mm(jnp.concatenate([rt[g] - qy[g][:, :MXU], gmat[g]], axis=0), s_ref[g], split_b=True) for g in G]
    for g in G:
        y_ref[:, sls[g]] = (ys[g][:L] + (av[g][L:] - qy[g][:, MXU:])).astype(y_ref.dtype)
        s_ref[g] = ys[g][L:] + hmat[g]


def _wkv(r, v, kk, lw, kd, bb, n_ctx_chunks):
    b, tc, dr = r.shape
    nch = tc // WKV_CHUNK

    def chunk(d, i):
        bwd = jnp.where(i < n_ctx_chunks, n_ctx_chunks - 1 - i, nch + n_ctx_chunks - 1 - i)
        return jnp.where(d == 0, i, bwd)

    one = pl.BlockSpec((None, WKV_CHUNK, dr), lambda bi, d, i: (bi, chunk(d, i), 0))
    two = pl.BlockSpec((None, None, WKV_CHUNK, dr), lambda bi, d, i: (d, bi, chunk(d, i), 0))
    return pl.pallas_call(
        functools.partial(_wkv_kernel, n_ctx_chunks),
        grid=(b, 2, nch),
        in_specs=[one, one, one, two, two, two],
        out_specs=two,
        out_shape=jax.ShapeDtypeStruct((2, b, tc, dr), BF16),
        scratch_shapes=[pltpu.VMEM((dr // MXU, MXU, MXU), F32)],
        compiler_params=_cparams(("parallel", "parallel", "arbitrary")),
        name="wkv",
    )(r, v, kk, lw, kd, bb)


def _cpow(lr, li, e, nbits):
    pr, pi = jnp.ones_like(lr), jnp.zeros_like(lr)
    sr, si = lr, li
    for bit in range(nbits):
        take = ((e >> bit) & 1) == 1
        pr, pi = jnp.where(take, pr * sr - pi * si, pr), jnp.where(take, pr * si + pi * sr, pi)
        if bit + 1 < nbits:
            sr, si = sr * sr - si * si, 2.0 * sr * si
    return pr, pi


def _zoh_lambda(ar, ai, ls):
    dt = jnp.exp(ls)
    mag = jnp.exp(ar * dt)
    ang = ai * dt
    return mag * jnp.cos(ang), mag * jnp.sin(ang)


def _s5w_kernel(ar_ref, ai_ref, ls_ref, br_ref, bi_ref, ctr_ref, cti_ref, cr_ref, ci_ref, arr_ref, air_ref, lsr_ref,
                wpt_ref, mt_ref, nf_ref, nb_ref, l16_ref):
    gb, p, w = ar_ref.shape[1:]
    n = S5_CHUNK
    lane_n = lax.broadcasted_iota(jnp.int32, (gb, p, w), 2) // S5_GROUP
    lane_n2 = lax.broadcasted_iota(jnp.int32, (S5_GROUP, w), 1) // S5_GROUP
    hp = lax.Precision.HIGHEST
    zpad = jnp.zeros((gb, LANE - p, w), wpt_ref.dtype)
    taps = []
    for d in range(2):
        ar, ai = ar_ref[d], ai_ref[d]
        lr, li = _zoh_lambda(ar, ai, ls_ref[d])
        den = ar * ar + ai * ai
        nr = lr - 1.0
        f_re = (nr * ar + li * ai) / den
        f_im = (li * ar - nr * ai) / den
        b_re, b_im = br_ref[d], bi_ref[d]
        bb_re = f_re * b_re - f_im * b_im
        bb_im = f_re * b_im + f_im * b_re
        pr, pi = _cpow(lr, li, (n - 1 - lane_n) if d == 0 else lane_n, 4)
        lb_re = pr * bb_re - pi * bb_im
        lb_im = pr * bb_im + pi * bb_re
        qr, qi = _cpow(lr, li, (lane_n + 1) if d == 0 else (n - lane_n), 5)
        c_re, c_im = ctr_ref[d], cti_ref[d]
        base = 2 * d * LANE
        wpt_ref[:, base:base + p, :] = lb_re.astype(wpt_ref.dtype)
        wpt_ref[:, base + p:base + LANE, :] = zpad
        wpt_ref[:, base + LANE:base + LANE + p, :] = lb_im.astype(wpt_ref.dtype)
        wpt_ref[:, base + LANE + p:base + 2 * LANE, :] = zpad
        n_ref = nf_ref if d == 0 else nb_ref
        n_ref[:, 0:p, :] = (c_re * qr - c_im * qi).astype(n_ref.dtype)
        n_ref[:, p:LANE, :] = zpad
        n_ref[:, LANE:LANE + p, :] = (-(c_re * qi + c_im * qr)).astype(n_ref.dtype)
        n_ref[:, LANE + p:2 * LANE, :] = zpad
        taps.append(jnp.einsum('ghp,gpn->ghn', cr_ref[d], lb_re, precision=hp, preferred_element_type=F32)
                    - jnp.einsum('ghp,gpn->ghn', ci_ref[d], lb_im, precision=hp, preferred_element_type=F32))
        lrr, lir = _zoh_lambda(arr_ref[d], air_ref[d], lsr_ref[d])
        for _ in range(int(math.log2(n))):
            lrr, lir = lrr * lrr - lir * lir, 2.0 * lrr * lir
        l16_ref[:, 2 * d:2 * d + 1, :] = lrr
        l16_ref[:, 2 * d + 1:2 * d + 2, :] = lir
    for g in range(gb):
        kf, kb = taps[0][g], taps[1][g]
        blocks = []
        for t in range(n):
            f = kf if t == n - 1 else pltpu.roll(kf, w - (n - 1 - t) * S5_GROUP, 1)
            r = kb if t == 0 else pltpu.roll(kb, t * S5_GROUP, 1)
            blocks.append(jnp.where(lane_n2 <= t, f, 0.0) + jnp.where(lane_n2 >= t, r, 0.0))
        mt_ref[g] = jnp.concatenate(blocks, axis=0).astype(mt_ref.dtype)


def _s5w(ar_t, ai_t, ls_t, br_t, bi_t, ctr_t, cti_t, c_re, c_im, ar_r, ai_r, ls_r):
    _, g, p, w = ar_t.shape
    gb = LANE // S5_GROUP
    big = pl.BlockSpec((2, gb, p, w), lambda j: (0, j, 0, 0))
    cs = pl.BlockSpec((2, gb, S5_GROUP, p), lambda j: (0, j, 0, 0))
    rs = pl.BlockSpec((2, gb, 1, LANE), lambda j: (0, j, 0, 0))
    tab = lambda rows: pl.BlockSpec((gb, rows, w), lambda j: (j, 0, 0))
    return pl.pallas_call(
        _s5w_kernel,
        grid=(g // gb,),
        in_specs=[big] * 7 + [cs, cs, rs, rs, rs],
        out_specs=[tab(4 * LANE), tab(w), tab(2 * LANE), tab(2 * LANE), pl.BlockSpec((gb, 4, LANE), lambda j: (j, 0, 0))],
        out_shape=[jax.ShapeDtypeStruct((g, 4 * LANE, w), BF16), jax.ShapeDtypeStruct((g, w, w), BF16),
                   jax.ShapeDtypeStruct((g, 2 * LANE, w), BF16), jax.ShapeDtypeStruct((g, 2 * LANE, w), BF16),
                   jax.ShapeDtypeStruct((g, 4, LANE), F32)],
        compiler_params=_cparams(("parallel",)),
        name="s5w",
    )(ar_t, ai_t, ls_t, br_t, bi_t, ctr_t, cti_t, c_re, c_im, ar_r, ai_r, ls_r)


def _s5_kernel(n_ctx_chunks, u_ref, wpt_ref, mt_ref, nf_ref, nb_ref, l16_ref, d_ref, y_ref,
               u_scr, e_ref, sf_ref, sb_ref):
    gb = wpt_ref.shape[0]
    nbs = u_ref.shape[0]
    nck = u_ref.shape[1] // S5_CHUNK
    lanes = gb * nbs
    halves = MXU // LANE
    lane_blk = lax.broadcasted_iota(jnp.int32, (nck, LANE), 1) // S5_GROUP

    for b in range(nbs):
        for hf in range(halves):
            xs = [u_ref[b, pl.ds(hf * gb + s8, nck, stride=S5_CHUNK), :] for s8 in range(gb)]
            for g in range(gb):
                acc = None
                for s8 in range(gb):
                    sh = ((s8 - g) % gb) * S5_GROUP
                    piece = pltpu.roll(xs[s8], sh, 1) if sh else xs[s8]
                    acc = piece if acc is None else jnp.where(lane_blk == s8, piece, acc)
                u_scr[g, b * nck:(b + 1) * nck, hf * LANE:(hf + 1) * LANE] = acc

    for g in range(gb):
        e = _mm(u_scr[g], wpt_ref[g], _NT)
        for b in range(nbs):
            for k in range(4):
                e_ref[k, pl.ds(g * nbs + b, nck, stride=lanes), :] = e[b * nck:(b + 1) * nck, k * LANE:(k + 1) * LANE]

    def lam_rows(k):
        return jnp.concatenate([jnp.broadcast_to(l16_ref[g, k:k + 1, :], (nbs, LANE)) for g in range(gb)], axis=0)

    lfr, lfi, lbr, lbi = (lam_rows(k) for k in range(4))
    zero = jnp.zeros((lanes, LANE), F32)

    def step(k, carry):
        fr, fi, br, bi = carry
        cb = jnp.where(k < n_ctx_chunks, n_ctx_chunks - 1 - k, nck - 1 + n_ctx_chunks - k)
        of = pl.multiple_of(k * lanes, lanes)
        ob = pl.multiple_of(cb * lanes, lanes)
        sf_ref[0, pl.ds(of, lanes), :] = fr
        sf_ref[1, pl.ds(of, lanes), :] = fi
        sb_ref[0, pl.ds(ob, lanes), :] = br
        sb_ref[1, pl.ds(ob, lanes), :] = bi
        efr, efi = e_ref[0, pl.ds(of, lanes), :], e_ref[1, pl.ds(of, lanes), :]
        ebr, ebi = e_ref[2, pl.ds(ob, lanes), :], e_ref[3, pl.ds(ob, lanes), :]
        return (lfr * fr - lfi * fi + efr, lfr * fi + lfi * fr + efi,
                lbr * br - lbi * bi + ebr, lbr * bi + lbi * br + ebi)

    lax.fori_loop(0, nck, step, (zero, zero, zero, zero))

    def rows_of(s_ref, g):
        return jnp.concatenate(
            [jnp.concatenate([s_ref[k, pl.ds(g * nbs + b, nck, stride=lanes), :] for k in range(2)], axis=1)
             for b in range(nbs)], axis=0)

    for g in range(gb):
        u = u_scr[g]
        y = _mm(u, mt_ref[g], _NT) + _mm(rows_of(sf_ref, g), nf_ref[g]) + _mm(rows_of(sb_ref, g), nb_ref[g])
        u_scr[g] = y + u * d_ref[g]

    for b in range(nbs):
        for hf in range(halves):
            ys = [u_scr[g, b * nck:(b + 1) * nck, hf * LANE:(hf + 1) * LANE] for g in range(gb)]
            for t8 in range(gb):
                acc = None
                for g in range(gb):
                    sh = ((g - t8) % gb) * S5_GROUP
                    piece = pltpu.roll(ys[g], sh, 1) if sh else ys[g]
                    acc = piece if acc is None else jnp.where(lane_blk == g, piece, acc)
                y_ref[b, pl.ds(hf * gb + t8, nck, stride=S5_CHUNK), :] = acc


def _s5(p, wpt, mt, nf, nbw, l16, d_t, ds, n_ctx_chunks):
    b, tc, npk = p.shape
    g, _, w = mt.shape
    gb = LANE // S5_GROUP
    nbs = 2 if b % 2 == 0 else 1
    nck = tc // S5_CHUNK
    u_blk0 = (npk - ds) // LANE
    grp = lambda shape: pl.BlockSpec((gb,) + shape, lambda j, bh: (j,) + (0,) * len(shape))
    return pl.pallas_call(
        functools.partial(_s5_kernel, n_ctx_chunks),
        grid=(g // gb, b // nbs),
        in_specs=[pl.BlockSpec((nbs, tc, LANE), lambda j, bh: (bh, 0, u_blk0 + j)),
                  grp((4 * LANE, w)), grp((w, w)), grp((2 * LANE, w)), grp((2 * LANE, w)), grp((4, LANE)),
                  grp((1, w))],
        out_specs=pl.BlockSpec((nbs, tc, LANE), lambda j, bh: (bh, 0, j)),
        out_shape=jax.ShapeDtypeStruct((b, tc, ds), F32),
        scratch_shapes=[pltpu.VMEM((gb, nbs * nck, w), F32),
                        pltpu.VMEM((4, nck * gb * nbs, LANE), F32),
                        pltpu.VMEM((2, nck * gb * nbs, LANE), F32),
                        pltpu.VMEM((2, nck * gb * nbs, LANE), F32)],
        compiler_params=_cparams(("parallel", "parallel")),
        name="s5",
    )(p, wpt, mt, nf, nbw, l16, d_t)


def _merge_kernel(yf_ref, yb_ref, bonus_ref, g_ref, y5_ref, x_ref, mod_ref, lnw_ref, lnb_ref, gluw_ref, glub_ref,
                  wout_ref, gffn_ref, x1_ref, h2_ref):
    ones = _head_ones()
    yh = yf_ref[...].astype(F32) + yb_ref[...].astype(F32)
    mean = _head_sum(yh, ones) * (1.0 / HEAD)
    cen = yh - mean
    var = _head_sum(cen * cen, ones) * (1.0 / HEAD)
    yn = cen * lax.rsqrt(var + LNX_EPS) * lnw_ref[...] + lnb_ref[...]
    o_rwkv = (yn + bonus_ref[...]) * g_ref[...]
    z = _gelu(y5_ref[...])
    o_s5 = z * _sigmoid(_mm(z, gluw_ref[...]) + glub_ref[...])
    o = jnp.concatenate([o_rwkv, o_s5], axis=1)
    mix = _mm(o, wout_ref[...])
    x1 = x_ref[...] + mod_ref[2:3, :] * mix
    x1_ref[...] = x1
    ms = jnp.mean(x1 * x1, axis=-1, keepdims=True)
    h = x1 * lax.rsqrt(ms + RMS_EPS) * gffn_ref[...]
    h2_ref[...] = (h * (1.0 + mod_ref[4:5, :]) + mod_ref[3:4, :]).astype(BF16)


def _merge(y, bonus, g, y5, x, mod3, lnx_w, lnx_b, glu_w, glu_b, w_out, g_ffn, n_ctx_tiles):
    b, t, d = x.shape
    dr = bonus.shape[-1]
    ds = y5.shape[-1]
    nt = t // TOK_TILE
    off = lambda bi, i: (bi, i + n_ctx_tiles, 0)
    const = lambda bi, i: (0, 0)
    return pl.pallas_call(
        _merge_kernel,
        grid=(b, nt),
        in_specs=[pl.BlockSpec((None, None, TOK_TILE, dr), lambda bi, i: (0, bi, i + n_ctx_tiles, 0)),
                  pl.BlockSpec((None, None, TOK_TILE, dr), lambda bi, i: (1, bi, i + n_ctx_tiles, 0)),
                  pl.BlockSpec((None, TOK_TILE, dr), off),
                  pl.BlockSpec((None, TOK_TILE, dr), off),
                  pl.BlockSpec((None, TOK_TILE, ds), off),
                  pl.BlockSpec((None, TOK_TILE, d), lambda bi, i: (bi, i, 0)),
                  pl.BlockSpec((None, 6, d), lambda bi, i: (bi, 0, 0)),
                  pl.BlockSpec((1, dr), const), pl.BlockSpec((1, dr), const),
                  pl.BlockSpec((ds, ds), const), pl.BlockSpec((1, ds), const),
                  pl.BlockSpec((dr + ds, d), const), pl.BlockSpec((1, d), const)],
        out_specs=[pl.BlockSpec((None, TOK_TILE, d), lambda bi, i: (bi, i, 0)),
                   pl.BlockSpec((None, TOK_TILE, d), lambda bi, i: (bi, i, 0))],
        out_shape=[jax.ShapeDtypeStruct((b, t, d), F32), jax.ShapeDtypeStruct((b, t, d), BF16)],
        compiler_params=_cparams(("parallel", "parallel")),
        name="merge",
    )(y, y, bonus, g, y5, x, mod3, lnx_w, lnx_b, glu_w, glu_b, w_out, g_ffn)


def _ffn_kernel(h_ref, wg_ref, wv_ref, cw_ref, cb_ref, wd_ref, x1_ref, mod_ref, gfin_ref, o_ref, acc_ref):
    f = pl.program_id(1)

    @pl.when(f == 0)
    def _():
        acc_ref[...] = jnp.zeros_like(acc_ref)

    h = h_ref[...]
    gate = jnp.dot(h, wg_ref[...], preferred_element_type=F32)
    val = jnp.dot(h, wv_ref[...], preferred_element_type=F32)
    prev, nxt = _shift_rows(gate)
    gate = cw_ref[0:1, :] * prev + cw_ref[1:2, :] * gate + cw_ref[2:3, :] * nxt + cb_ref[...]
    act = (_gelu(gate) * val).astype(BF16)
    acc_ref[...] += jnp.dot(act, wd_ref[...], preferred_element_type=F32)

    @pl.when(f == pl.num_programs(1) - 1)
    def _():
        x2 = x1_ref[...] + mod_ref[5:6, :] * acc_ref[...]
        ms = jnp.mean(x2 * x2, axis=-1, keepdims=True)
        o_ref[...] = x2 * lax.rsqrt(ms + RMS_EPS) * gfin_ref[...]


def _ffn(h2, w_up, conv_w, conv_b, w_down, x1, mod3, g_fin):
    b, t, d = x1.shape
    ff = w_down.shape[0]
    tm = min(FFN_TOK_TILE, t)
    tf = min(FFN_F_TILE, ff)
    nt, nf = t // tm, ff // tf
    return pl.pallas_call(
        _ffn_kernel,
        grid=(b * nt, nf),
        in_specs=[pl.BlockSpec((None, tm, d), lambda i, f: (i // nt, i % nt, 0)),
                  pl.BlockSpec((d, tf), lambda i, f: (0, f)),
                  pl.BlockSpec((d, tf), lambda i, f: (0, nf + f)),
                  pl.BlockSpec((3, tf), lambda i, f: (0, f)),
                  pl.BlockSpec((1, tf), lambda i, f: (0, f)),
                  pl.BlockSpec((tf, d), lambda i, f: (f, 0)),
                  pl.BlockSpec((None, tm, d), lambda i, f: (i // nt, i % nt, 0)),
                  pl.BlockSpec((None, 6, d), lambda i, f: (i // nt, 0, 0)),
                  pl.BlockSpec((1, d), lambda i, f: (0, 0))],
        out_specs=pl.BlockSpec((None, tm, d), lambda i, f: (i // nt, i % nt, 0)),
        out_shape=jax.ShapeDtypeStruct((b, t, d), F32),
        scratch_shapes=[pltpu.VMEM((tm, d), F32)],
        compiler_params=_cparams(("parallel", "arbitrary")),
        name="ffn",
    )(h2, w_up, w_up, conv_w, conv_b, w_down, x1, mod3, g_fin)


def _pad_cols(a, n):
    return jnp.pad(a, [(0, 0)] * (a.ndim - 1) + [(0, n - a.shape[-1])])


def _pack_in(a, dr, sizes, pads):
    gl, dl, al = sizes
    glp, wlp, alp = pads
    o = 3 * dr
    parts = [a[..., :o], _pad_cols(a[..., o:o + gl], glp), _pad_cols(a[..., o + gl:o + gl + 2 * dl], wlp),
             _pad_cols(a[..., o + gl + 2 * dl:o + gl + 2 * dl + 2 * al], alp), a[..., o + gl + 2 * dl + 2 * al:]]
    return jnp.concatenate(parts, axis=-1)


def _lora_up(w, rows):
    _, r, dr = w.shape
    out = jnp.zeros((rows, 2 * dr), w.dtype)
    out = out.at[0:r, 0:dr].set(w[0])
    out = out.at[r:2 * r, dr:2 * dr].set(w[1])
    return out


def kernel(x, c, ctx, c_ctx, mod_w, mod_b, norm_mix_g, w_in, shift_mu, rwkv_w0, rwkv_w_up, rwkv_a0, rwkv_a_up,
           rwkv_g_up, rwkv_k_k, rwkv_k_a, rwkv_r_k, lnx_w, lnx_b, s5_a_re, s5_a_im, s5_log_step, s5_b_re, s5_b_im,
           s5_c_re, s5_c_im, s5_d, s5_glu_w, s5_glu_b, w_out, norm_ffn_g, ffn_w_up, ffn_conv_w, ffn_conv_b,
           ffn_w_down, final_norm_g):
    assert mod_w.shape[0] == 1, "single trunk layer"
    b, t, d = x.shape
    ctx_len = ctx.shape[1]
    dr = rwkv_w0.shape[-1]
    ds = s5_d.shape[-1]
    dl, al, gl = rwkv_w_up.shape[2], rwkv_a_up.shape[2], rwkv_g_up.shape[1]
    n_groups = ds // S5_GROUP
    assert ctx_len == TOK_TILE and t % FFN_TOK_TILE == 0
    assert dr % MXU == 0 and s5_a_re.shape[-1] == S5_STATE and S5_CHUNK * S5_GROUP == MXU
    glp, wlp, alp = _round_up(gl, LANE), _round_up(2 * dl, LANE), _round_up(2 * al, LANE)
    dims = (dr, glp, wlp, alp)
    tc = ctx_len + t

    rows = _round_up(b + 1, 8)
    c_all = jnp.zeros((rows, d), F32).at[:b].set(c).at[b].set(c_ctx)
    mod3 = _mod(c_all, mod_w[0], mod_b).reshape(rows, 6, d)

    w_packed = _pack_in(w_in[0], dr, (gl, dl, al), (glp, wlp, alp)).astype(BF16)
    mu_p = _pack_in(shift_mu[0], dr, (gl, dl, al), (glp, wlp, alp))
    p = _inproj(ctx, x, mod3, norm_mix_g, w_packed, b)
    r, v, kk, lw, kd, bb, bonus, gate = _prep(
        p, mu_p, rwkv_w0[0], rwkv_a0[0], rwkv_k_k, rwkv_k_a, rwkv_r_k[0].reshape(1, dr),
        _lora_up(rwkv_w_up[0], wlp).astype(BF16), _lora_up(rwkv_a_up[0], alp).astype(BF16),
        jnp.pad(rwkv_g_up[0], [(0, glp - gl), (0, 0)]).astype(BF16), dims)

    y = _wkv(r, v, kk, lw, kd, bb, ctx_len // WKV_CHUNK)

    w = S5_CHUNK * S5_GROUP
    tile_col = lambda a: jnp.broadcast_to(a[..., None], a.shape + (w,))
    ls_t = jnp.broadcast_to(s5_log_step[0][:, :, None, None], (2, n_groups, S5_STATE, w))
    b_t = lambda a: jnp.tile(a, (1, 1, 1, S5_CHUNK))
    c_t = lambda a: jnp.tile(a.transpose(0, 1, 3, 2), (1, 1, 1, S5_CHUNK))
    row = lambda a: jnp.pad(a[:, :, None, :], [(0, 0), (0, 0), (0, 0), (0, LANE - S5_STATE)])
    ls_r = jnp.broadcast_to(s5_log_step[0][:, :, None, None], (2, n_groups, 1, LANE))
    wpt, mt, nf, nbw, l16 = _s5w(tile_col(s5_a_re[0]), tile_col(s5_a_im[0]), ls_t, b_t(s5_b_re[0]), b_t(s5_b_im[0]),
                                 c_t(s5_c_re[0]), c_t(s5_c_im[0]), s5_c_re[0], s5_c_im[0],
                                 row(s5_a_re[0]), row(s5_a_im[0]), ls_r)
    d_t = jnp.tile(s5_d[0].reshape(n_groups, 1, S5_GROUP), (1, 1, S5_CHUNK))
    y5 = _s5(p, wpt, mt, nf, nbw, l16, d_t, ds, ctx_len // S5_CHUNK)

    x1, h2 = _merge(y, bonus, gate, y5, x, mod3, lnx_w, lnx_b, s5_glu_w[0].astype(BF16), s5_glu_b,
                    w_out[0].astype(BF16), norm_ffn_g, ctx_len // TOK_TILE)
    return _ffn(h2, ffn_w_up[0].astype(BF16), ffn_conv_w[0], ffn_conv_b, ffn_w_down[0].astype(BF16), x1, mod3,
                final_norm_g.reshape(1, d))
```

```python
import functools
import math

import jax
import jax.numpy as jnp
from jax import lax
from jax.experimental import pallas as pl
from jax.experimental.pallas import tpu as pltpu

F32 = jnp.float32
BF16 = jnp.bfloat16

HEAD = 64
GRID_W = 64
S5_GROUP = 16
S5_STATE = 64
RMS_EPS = 1e-6
LNX_EPS = 64e-5
LANE = 128
MXU = 256
HEADS_PER_GROUP = MXU // HEAD
WKV_CHUNK = 64
S5_CHUNK = 16
TOK_TILE = 256
FFN_TOK_TILE = 512
FFN_F_TILE = 512
VMEM_LIMIT = 56 * 1024 * 1024


def _cparams(sem):
    return pltpu.CompilerParams(dimension_semantics=sem, vmem_limit_bytes=VMEM_LIMIT)


def _round_up(n, m):
    return (n + m - 1) // m * m


def _split(a):
    hi = a.astype(BF16)
    lo = (a - hi.astype(F32)).astype(BF16)
    return hi, lo


def _mm(a, b, dims=(((1,), (0,)), ((), ())), split_a=False, split_b=False):
    dot = functools.partial(lax.dot_general, dimension_numbers=dims, preferred_element_type=F32)
    if not split_a and not split_b:
        return dot(a.astype(BF16), b.astype(BF16))
    a_hi, a_lo = _split(a) if split_a else (a.astype(BF16), None)
    b_hi, b_lo = _split(b) if split_b else (b.astype(BF16), None)
    out = dot(a_hi, b_hi)
    if a_lo is not None:
        out = out + dot(a_lo, b_hi)
    if b_lo is not None:
        out = out + dot(a_hi, b_lo)
    return out


_NT = (((1,), (1,)), ((), ()))
_TN = (((0,), (0,)), ((), ()))


def _sigmoid(x):
    return 1.0 / (1.0 + jnp.exp(-x))


def _gelu(x):
    return 0.5 * x * (1.0 + jnp.tanh(math.sqrt(2.0 / math.pi) * (x + 0.044715 * (x * x * x))))


def _head_ones():
    r = lax.broadcasted_iota(jnp.int32, (MXU, MXU), 0) // HEAD
    c = lax.broadcasted_iota(jnp.int32, (MXU, MXU), 1) // HEAD
    return jnp.where(r == c, 1.0, 0.0).astype(BF16)


def _head_sum(x, ones):
    parts = []
    for g in range(x.shape[1] // MXU):
        parts.append(_mm(x[:, g * MXU:(g + 1) * MXU], ones, split_a=True))
    return parts[0] if len(parts) == 1 else jnp.concatenate(parts, axis=1)


def _shift_rows(x, whole_tile=False):
    n = x.shape[0]
    row = lax.broadcasted_iota(jnp.int32, (n, 1), 0)
    pos = row & (GRID_W - 1)
    first = jnp.where(whole_tile, row, pos) == 0
    last = jnp.where(whole_tile, row - (n - 1), pos - (GRID_W - 1)) == 0
    prev = jnp.where(first, 0.0, pltpu.roll(x, 1, 0))
    nxt = jnp.where(last, 0.0, pltpu.roll(x, n - 1, 0))
    return prev, nxt


def _mod_kernel(c_ref, w_ref, b_ref, o_ref):
    c = c_ref[...]
    s = c * _sigmoid(c)
    o_ref[...] = lax.dot_general(s, w_ref[...], (((1,), (0,)), ((), ())), precision=lax.Precision.HIGHEST,
                                 preferred_element_type=F32) + b_ref[...]


def _mod(c_all, mod_w, mod_b):
    rows, d = c_all.shape
    n = mod_w.shape[1]
    tn = 1024
    return pl.pallas_call(
        _mod_kernel,
        grid=(n // tn,),
        in_specs=[pl.BlockSpec((rows, d), lambda j: (0, 0)),
                  pl.BlockSpec((d, tn), lambda j: (0, j)),
                  pl.BlockSpec((1, tn), lambda j: (0, j))],
        out_specs=pl.BlockSpec((rows, tn), lambda j: (0, j)),
        out_shape=jax.ShapeDtypeStruct((rows, n), F32),
        compiler_params=_cparams(("parallel",)),
        name="mod",
    )(c_all, mod_w, mod_b)


def _inproj_kernel(c_ref, x_ref, mod_ref, g_ref, w_ref, o_ref):
    x = jnp.where(pl.program_id(1) == 0, c_ref[...], x_ref[...])
    ms = jnp.mean(x * x, axis=-1, keepdims=True)
    h = x * lax.rsqrt(ms + RMS_EPS) * g_ref[...]
    h = h * (1.0 + mod_ref[1:2, :]) + mod_ref[0:1, :]
    o_ref[...] = jnp.dot(h.astype(BF16), w_ref[...], preferred_element_type=F32)


def _inproj(ctx, x, mod3, g, w_packed, n_batch):
    b, t, d = x.shape
    npk = w_packed.shape[1]
    tc = ctx.shape[1] + t
    nt = tc // TOK_TILE
    return pl.pallas_call(
        _inproj_kernel,
        grid=(b, nt),
        in_specs=[pl.BlockSpec((None, TOK_TILE, d), lambda bi, i: (bi, 0, 0)),
                  pl.BlockSpec((None, TOK_TILE, d), lambda bi, i: (bi, jnp.maximum(i - 1, 0), 0)),
                  pl.BlockSpec((None, 6, d), lambda bi, i: (jnp.where(i == 0, n_batch, bi), 0, 0)),
                  pl.BlockSpec((1, d), lambda bi, i: (0, 0)),
                  pl.BlockSpec((d, npk), lambda bi, i: (0, 0), pipeline_mode=pl.Buffered(1))],
        out_specs=pl.BlockSpec((None, TOK_TILE, npk), lambda bi, i: (bi, i, 0)),
        out_shape=jax.ShapeDtypeStruct((b, tc, npk), F32),
        compiler_params=_cparams(("parallel", "parallel")),
        name="inproj",
    )(ctx, x, mod3, g, w_packed)


def _prep_kernel(dims, p_ref, mu_ref, w0_ref, a0_ref, kk_w_ref, ka_ref, rk_ref, wup_ref, aup_ref, gup_ref,
                 r_ref, v_ref, kk_ref, lw_ref, kd_ref, bb_ref, bonus_ref, g_ref):
    dr, glp, wlp, alp = dims
    nq = 3 * dr + glp + wlp + alp
    q = p_ref[:, :nq]
    prev, nxt = _shift_rows(q, whole_tile=pl.program_id(1) == 0)
    q = q + mu_ref[0:1, :] * (prev - q) + mu_ref[1:2, :] * (nxt - q)
    r = q[:, 0:dr]
    k = q[:, dr:2 * dr]
    v = q[:, 2 * dr:3 * dr]
    o = 3 * dr
    gd = q[:, o:o + glp]
    wd = q[:, o + glp:o + glp + wlp]
    ad = q[:, o + glp + wlp:o + glp + wlp + alp]

    ones = _head_ones()
    kraw = k * kk_w_ref[...]
    kk = kraw / jnp.maximum(jnp.sqrt(_head_sum(kraw * kraw, ones)), 1e-12)
    lora_w = _mm(jnp.tanh(wd), wup_ref[...], split_a=True)
    lora_a = _mm(ad, aup_ref[...], split_a=True)
    r_ref[...] = r.astype(r_ref.dtype)
    v_ref[...] = v.astype(v_ref.dtype)
    kk_ref[...] = kk.astype(kk_ref.dtype)
    for d in range(2):
        z = w0_ref[d:d + 1, :] + lora_w[:, d * dr:(d + 1) * dr]
        lw_ref[d] = -math.exp(-0.5) * _sigmoid(z)
        a = _sigmoid(a0_ref[d:d + 1, :] + lora_a[:, d * dr:(d + 1) * dr])
        kd_ref[d] = (k * (1.0 + (a - 1.0) * ka_ref[...])).astype(kd_ref.dtype)
        bb_ref[d] = (kk * a).astype(bb_ref.dtype)
    bonus_ref[...] = (_head_sum(r * k * rk_ref[...], ones) * v).astype(bonus_ref.dtype)
    g_ref[...] = _mm(_sigmoid(gd), gup_ref[...]).astype(g_ref.dtype)


def _prep(p, mu_p, w0, a0, k_k, k_a, r_k, wup_c, aup_c, gup_p, dims):
    b, tc, npk = p.shape
    dr, glp, wlp, alp = dims
    nq = 3 * dr + glp + wlp + alp
    nt = tc // TOK_TILE
    tok = lambda bi, i: (bi, i, 0)
    tok2 = lambda bi, i: (0, bi, i, 0)
    const = lambda bi, i: (0, 0)
    one = jax.ShapeDtypeStruct((b, tc, dr), BF16)
    two = jax.ShapeDtypeStruct((2, b, tc, dr), BF16)
    lws = jax.ShapeDtypeStruct((2, b, tc, dr), F32)
    spec1 = pl.BlockSpec((None, TOK_TILE, dr), tok)
    spec2 = pl.BlockSpec((2, None, TOK_TILE, dr), tok2)
    return pl.pallas_call(
        functools.partial(_prep_kernel, dims),
        grid=(b, nt),
        in_specs=[pl.BlockSpec((None, TOK_TILE, npk), tok),
                  pl.BlockSpec((2, nq), const),
                  pl.BlockSpec((2, dr), const), pl.BlockSpec((2, dr), const),
                  pl.BlockSpec((1, dr), const), pl.BlockSpec((1, dr), const), pl.BlockSpec((1, dr), const),
                  pl.BlockSpec((wlp, 2 * dr), const), pl.BlockSpec((alp, 2 * dr), const),
                  pl.BlockSpec((glp, dr), const)],
        out_specs=[spec1, spec1, spec1, spec2, spec2, spec2, spec1, spec1],
        out_shape=[one, one, one, lws, two, two, one, one],
        compiler_params=_cparams(("parallel", "parallel")),
        name="prep",
    )(p, mu_p, w0, a0, k_k, k_a, r_k, wup_c, aup_c, gup_p)


def _bd(x, bdmask):
    return jnp.where(bdmask, jnp.concatenate([x] * HEADS_PER_GROUP, axis=0), 0.0)


def _wkv_kernel(r0_ref, v0_ref, kk0_ref, r1_ref, v1_ref, kk1_ref, lw0_ref, kd0_ref, bb0_ref, lw1_ref, kd1_ref, bb1_ref,
                y0_ref, y1_ref, s_ref):
    i = pl.program_id(1)
    L = WKV_CHUNK
    n_groups = r0_ref.shape[1] // MXU
    dirs = ((r0_ref, v0_ref, kk0_ref, lw0_ref, kd0_ref, bb0_ref, y0_ref),
            (r1_ref, v1_ref, kk1_ref, lw1_ref, kd1_ref, bb1_ref, y1_ref))

    @pl.when(i == 0)
    def _():
        s_ref[...] = jnp.zeros_like(s_ref)

    row = lax.broadcasted_iota(jnp.int32, (L, 1), 0)
    t_idx = lax.broadcasted_iota(jnp.int32, (L, MXU), 0)
    s_idx = lax.broadcasted_iota(jnp.int32, (L, MXU), 1) % HEAD
    eye_sbs = jnp.where(t_idx == s_idx, 1.0, 0.0)
    br = lax.broadcasted_iota(jnp.int32, (MXU, MXU), 0)
    bc = lax.broadcasted_iota(jnp.int32, (MXU, MXU), 1)
    bdmask = (br // HEAD) == (bc // HEAD)
    eye_bd = jnp.where(br == bc, 1.0, 0.0)

    G = range(2 * n_groups)
    sls = [slice(g * MXU, (g + 1) * MXU) for g in range(n_groups)] * 2
    outs = [dirs[0][6]] * n_groups + [dirs[1][6]] * n_groups
    kt, bt, kdt, rt, vv, p_end, strict, incl = [], [], [], [], [], [], [], []
    for d, (r_ref, v_ref, kk_ref, lw_ref, kd_ref, bb_ref, _) in enumerate(dirs):
        lw = lw_ref[...]
        pre = lw
        shift = 1
        while shift < L:
            pre = pre + jnp.where(row >= shift, pltpu.roll(pre, shift, 0), 0.0)
            shift *= 2
        cum = pre if d == 0 else pre[L - 1:L, :] - pre + lw
        for sl in sls[:n_groups]:
            c = cum[:, sl]
            p_inv = jnp.exp(-c)
            kt.append(kk_ref[:, sl].astype(F32) * jnp.exp(c - lw[:, sl]))
            bt.append(bb_ref[:, sl].astype(F32) * p_inv)
            kdt.append(kd_ref[:, sl].astype(F32) * p_inv)
            rt.append(r_ref[:, sl].astype(F32) * jnp.exp(c))
            vv.append(v_ref[:, sl].astype(F32))
            p_end.append(jnp.exp(c[L - 1:L, :] if d == 0 else c[0:1, :]))
            strict.append(t_idx > s_idx if d == 0 else t_idx < s_idx)
            incl.append(t_idx >= s_idx if d == 0 else t_idx <= s_idx)

    a_all = [_mm(jnp.concatenate([kt[g], rt[g]], axis=0),
                 jnp.concatenate([_bd(bt[g], bdmask), _bd(kdt[g], bdmask)], axis=0), _NT) for g in G]
    a_ab = [jnp.where(strict[g], a_all[g][:L, :MXU], 0.0) for g in G]
    a_ak = [jnp.where(strict[g], a_all[g][:L, MXU:], 0.0) for g in G]
    a_rb = [jnp.where(incl[g], a_all[g][L:, :MXU], 0.0) for g in G]
    a_rk = [jnp.where(incl[g], a_all[g][L:, MXU:], 0.0) for g in G]
    av = [_mm(jnp.concatenate([a_ak[g], a_rk[g]], axis=0), _bd(vv[g], bdmask)) for g in G]

    npow = [_mm(n, _bd(n, bdmask)) for n in a_ab]
    tinv = [eye_sbs - a for a in a_ab]
    for _ in range(int(math.log2(L)) - 2):
        both = [_mm(jnp.concatenate([n, t], axis=0), _bd(n, bdmask)) for n, t in zip(npow, tinv)]
        npow = [x[:L] for x in both]
        tinv = [t + x[L:] for t, x in zip(tinv, both)]
    tinv = [t + _mm(t, _bd(n, bdmask)) for t, n in zip(tinv, npow)]

    wu = [_mm(tinv[g], jnp.concatenate([_bd(kt[g], bdmask), _bd(av[g][:L], bdmask)], axis=1)) for g in G]
    wm = [w[:, :MXU] for w in wu]
    um = [w[:, MXU:] for w in wu]
    qy = [_mm(a_rb[g], jnp.concatenate([_bd(wm[g], bdmask), _bd(um[g], bdmask)], axis=1)) for g in G]
    btp = [bt[g] * p_end[g] for g in G]
    gmat = [jnp.where(bdmask, eye_bd * p_end[g] - _mm(btp[g], wm[g], _TN), 0.0) for g in G]
    hmat = [jnp.where(bdmask, _mm(jnp.concatenate([kdt[g] * p_end[g], btp[g]], axis=0),
                                  jnp.concatenate([vv[g], -um[g]], axis=0), _TN), 0.0) for g in G]
    ys = [_mm(jnp.concatenate([rt[g] - qy[g][:, :MXU], gmat[g]], axis=0), s_ref[g]) for g in G]
    for g in G:
        outs[g][:, sls[g]] = (ys[g][:L] + (av[g][L:] - qy[g][:, MXU:])).astype(outs[g].dtype)
        s_ref[g] = ys[g][L:] + hmat[g]


def _wkv(r, v, kk, lw, kd, bb, n_ctx_chunks):
    b, tc, dr = r.shape
    nch = tc // WKV_CHUNK

    def rev(i):
        return jnp.where(i < n_ctx_chunks, n_ctx_chunks - 1 - i, nch + n_ctx_chunks - 1 - i)

    one0 = pl.BlockSpec((None, WKV_CHUNK, dr), lambda bi, i: (bi, i, 0))
    one1 = pl.BlockSpec((None, WKV_CHUNK, dr), lambda bi, i: (bi, rev(i), 0))
    two0 = pl.BlockSpec((None, None, WKV_CHUNK, dr), lambda bi, i: (0, bi, i, 0))
    two1 = pl.BlockSpec((None, None, WKV_CHUNK, dr), lambda bi, i: (1, bi, rev(i), 0))
    y_shape = jax.ShapeDtypeStruct((b, tc, dr), BF16)
    return pl.pallas_call(
        _wkv_kernel,
        grid=(b, nch),
        in_specs=[one0, one0, one0, one1, one1, one1, two0, two0, two0, two1, two1, two1],
        out_specs=[one0, one1],
        out_shape=[y_shape, y_shape],
        scratch_shapes=[pltpu.VMEM((2 * dr // MXU, MXU, MXU), F32)],
        compiler_params=_cparams(("parallel", "arbitrary")),
        name="wkv",
    )(r, v, kk, r, v, kk, lw, kd, bb, lw, kd, bb)


def _cpow(lr, li, e, nbits):
    pr, pi = jnp.ones_like(lr), jnp.zeros_like(lr)
    sr, si = lr, li
    for bit in range(nbits):
        take = ((e >> bit) & 1) == 1
        pr, pi = jnp.where(take, pr * sr - pi * si, pr), jnp.where(take, pr * si + pi * sr, pi)
        if bit + 1 < nbits:
            sr, si = sr * sr - si * si, 2.0 * sr * si
    return pr, pi


def _zoh_lambda(ar, ai, ls):
    dt = jnp.exp(ls)
    mag = jnp.exp(ar * dt)
    ang = ai * dt
    return mag * jnp.cos(ang), mag * jnp.sin(ang)


def _s5w_kernel(ar_ref, ai_ref, ls_ref, br_ref, bi_ref, ctr_ref, cti_ref, cr_ref, ci_ref, arr_ref, air_ref, lsr_ref,
                wpt_ref, mt_ref, nf_ref, nb_ref, l16_ref):
    gb, p, w = ar_ref.shape[1:]
    n = S5_CHUNK
    lane_n = lax.broadcasted_iota(jnp.int32, (gb, p, w), 2) // S5_GROUP
    lane_n2 = lax.broadcasted_iota(jnp.int32, (S5_GROUP, w), 1) // S5_GROUP
    hp = lax.Precision.HIGHEST
    zpad = jnp.zeros((gb, LANE - p, w), wpt_ref.dtype)
    taps = []
    for d in range(2):
        ar, ai = ar_ref[d], ai_ref[d]
        lr, li = _zoh_lambda(ar, ai, ls_ref[d])
        den = ar * ar + ai * ai
        nr = lr - 1.0
        f_re = (nr * ar + li * ai) / den
        f_im = (li * ar - nr * ai) / den
        b_re, b_im = br_ref[d], bi_ref[d]
        bb_re = f_re * b_re - f_im * b_im
        bb_im = f_re * b_im + f_im * b_re
        pr, pi = _cpow(lr, li, (n - 1 - lane_n) if d == 0 else lane_n, 4)
        lb_re = pr * bb_re - pi * bb_im
        lb_im = pr * bb_im + pi * bb_re
        qr, qi = _cpow(lr, li, (lane_n + 1) if d == 0 else (n - lane_n), 5)
        c_re, c_im = ctr_ref[d], cti_ref[d]
        base = 2 * d * LANE
        wpt_ref[:, base:base + p, :] = lb_re.astype(wpt_ref.dtype)
        wpt_ref[:, base + p:base + LANE, :] = zpad
        wpt_ref[:, base + LANE:base + LANE + p, :] = lb_im.astype(wpt_ref.dtype)
        wpt_ref[:, base + LANE + p:base + 2 * LANE, :] = zpad
        n_ref = nf_ref if d == 0 else nb_ref
        n_ref[:, 0:p, :] = (c_re * qr - c_im * qi).astype(n_ref.dtype)
        n_ref[:, p:LANE, :] = zpad
        n_ref[:, LANE:LANE + p, :] = (-(c_re * qi + c_im * qr)).astype(n_ref.dtype)
        n_ref[:, LANE + p:2 * LANE, :] = zpad
        taps.append(jnp.einsum('ghp,gpn->ghn', cr_ref[d], lb_re, precision=hp, preferred_element_type=F32)
                    - jnp.einsum('ghp,gpn->ghn', ci_ref[d], lb_im, precision=hp, preferred_element_type=F32))
        lrr, lir = _zoh_lambda(arr_ref[d], air_ref[d], lsr_ref[d])
        for _ in range(int(math.log2(n))):
            lrr, lir = lrr * lrr - lir * lir, 2.0 * lrr * lir
        l16_ref[:, 2 * d:2 * d + 1, :] = lrr
        l16_ref[:, 2 * d + 1:2 * d + 2, :] = lir
    for g in range(gb):
        kf, kb = taps[0][g], taps[1][g]
        blocks = []
        for t in range(n):
            f = kf if t == n - 1 else pltpu.roll(kf, w - (n - 1 - t) * S5_GROUP, 1)
            r = kb if t == 0 else pltpu.roll(kb, t * S5_GROUP, 1)
            blocks.append(jnp.where(lane_n2 <= t, f, 0.0) + jnp.where(lane_n2 >= t, r, 0.0))
        mt_ref[g] = jnp.concatenate(blocks, axis=0).astype(mt_ref.dtype)


def _s5w(ar_t, ai_t, ls_t, br_t, bi_t, ctr_t, cti_t, c_re, c_im, ar_r, ai_r, ls_r):
    _, g, p, w = ar_t.shape
    gb = LANE // S5_GROUP
    big = pl.BlockSpec((2, gb, p, w), lambda j: (0, j, 0, 0))
    cs = pl.BlockSpec((2, gb, S5_GROUP, p), lambda j: (0, j, 0, 0))
    rs = pl.BlockSpec((2, gb, 1, LANE), lambda j: (0, j, 0, 0))
    tab = lambda rows: pl.BlockSpec((gb, rows, w), lambda j: (j, 0, 0))
    return pl.pallas_call(
        _s5w_kernel,
        grid=(g // gb,),
        in_specs=[big] * 7 + [cs, cs, rs, rs, rs],
        out_specs=[tab(4 * LANE), tab(w), tab(2 * LANE), tab(2 * LANE), pl.BlockSpec((gb, 4, LANE), lambda j: (j, 0, 0))],
        out_shape=[jax.ShapeDtypeStruct((g, 4 * LANE, w), BF16), jax.ShapeDtypeStruct((g, w, w), BF16),
                   jax.ShapeDtypeStruct((g, 2 * LANE, w), BF16), jax.ShapeDtypeStruct((g, 2 * LANE, w), BF16),
                   jax.ShapeDtypeStruct((g, 4, LANE), F32)],
        compiler_params=_cparams(("parallel",)),
        name="s5w",
    )(ar_t, ai_t, ls_t, br_t, bi_t, ctr_t, cti_t, c_re, c_im, ar_r, ai_r, ls_r)


def _s5_kernel(n_ctx_chunks, u_ref, wpt_ref, mt_ref, nf_ref, nb_ref, l16_ref, d_ref, y_ref,
               u_scr, e_ref, sf_ref, sb_ref):
    gb = wpt_ref.shape[0]
    nbs = u_ref.shape[0]
    nck = u_ref.shape[1] // S5_CHUNK
    lanes = gb * nbs
    halves = MXU // LANE
    lane_blk = lax.broadcasted_iota(jnp.int32, (nck, LANE), 1) // S5_GROUP

    for b in range(nbs):
        for hf in range(halves):
            xs = [u_ref[b, pl.ds(hf * gb + s8, nck, stride=S5_CHUNK), :] for s8 in range(gb)]
            for g in range(gb):
                acc = None
                for s8 in range(gb):
                    sh = ((s8 - g) % gb) * S5_GROUP
                    piece = pltpu.roll(xs[s8], sh, 1) if sh else xs[s8]
                    acc = piece if acc is None else jnp.where(lane_blk == s8, piece, acc)
                u_scr[g, b * nck:(b + 1) * nck, hf * LANE:(hf + 1) * LANE] = acc

    for g in range(gb):
        e = _mm(u_scr[g], wpt_ref[g], _NT)
        for b in range(nbs):
            for k in range(4):
                e_ref[k, pl.ds(g * nbs + b, nck, stride=lanes), :] = e[b * nck:(b + 1) * nck, k * LANE:(k + 1) * LANE]

    def lam_rows(k):
        return jnp.concatenate([jnp.broadcast_to(l16_ref[g, k:k + 1, :], (nbs, LANE)) for g in range(gb)], axis=0)

    lfr, lfi, lbr, lbi = (lam_rows(k) for k in range(4))
    zero = jnp.zeros((lanes, LANE), F32)

    def step(k, carry):
        fr, fi, br, bi = carry
        cb = jnp.where(k < n_ctx_chunks, n_ctx_chunks - 1 - k, nck - 1 + n_ctx_chunks - k)
        of = pl.multiple_of(k * lanes, lanes)
        ob = pl.multiple_of(cb * lanes, lanes)
        sf_ref[0, pl.ds(of, lanes), :] = fr
        sf_ref[1, pl.ds(of, lanes), :] = fi
        sb_ref[0, pl.ds(ob, lanes), :] = br
        sb_ref[1, pl.ds(ob, lanes), :] = bi
        efr, efi = e_ref[0, pl.ds(of, lanes), :], e_ref[1, pl.ds(of, lanes), :]
        ebr, ebi = e_ref[2, pl.ds(ob, lanes), :], e_ref[3, pl.ds(ob, lanes), :]
        return (lfr * fr - lfi * fi + efr, lfr * fi + lfi * fr + efi,
                lbr * br - lbi * bi + ebr, lbr * bi + lbi * br + ebi)

    lax.fori_loop(0, nck, step, (zero, zero, zero, zero))

    def rows_of(s_ref, g):
        return jnp.concatenate(
            [jnp.concatenate([s_ref[k, pl.ds(g * nbs + b, nck, stride=lanes), :] for k in range(2)], axis=1)
             for b in range(nbs)], axis=0)

    for g in range(gb):
        u = u_scr[g]
        y = _mm(u, mt_ref[g], _NT) + _mm(rows_of(sf_ref, g), nf_ref[g]) + _mm(rows_of(sb_ref, g), nb_ref[g])
        u_scr[g] = y + u * d_ref[g]

    for b in range(nbs):
        for hf in range(halves):
            ys = [u_scr[g, b * nck:(b + 1) * nck, hf * LANE:(hf + 1) * LANE] for g in range(gb)]
            for t8 in range(gb):
                acc = None
                for g in range(gb):
                    sh = ((g - t8) % gb) * S5_GROUP
                    piece = pltpu.roll(ys[g], sh, 1) if sh else ys[g]
                    acc = piece if acc is None else jnp.where(lane_blk == g, piece, acc)
                y_ref[b, pl.ds(hf * gb + t8, nck, stride=S5_CHUNK), :] = acc


def _s5(p, wpt, mt, nf, nbw, l16, d_t, ds, n_ctx_chunks):
    b, tc, npk = p.shape
    g, _, w = mt.shape
    gb = LANE // S5_GROUP
    nbs = 2 if b % 2 == 0 else 1
    nck = tc // S5_CHUNK
    u_blk0 = (npk - ds) // LANE
    grp = lambda shape: pl.BlockSpec((gb,) + shape, lambda j, bh: (j,) + (0,) * len(shape))
    return pl.pallas_call(
        functools.partial(_s5_kernel, n_ctx_chunks),
        grid=(g // gb, b // nbs),
        in_specs=[pl.BlockSpec((nbs, tc, LANE), lambda j, bh: (bh, 0, u_blk0 + j)),
                  grp((4 * LANE, w)), grp((w, w)), grp((2 * LANE, w)), grp((2 * LANE, w)), grp((4, LANE)),
                  grp((1, w))],
        out_specs=pl.BlockSpec((nbs, tc, LANE), lambda j, bh: (bh, 0, j)),
        out_shape=jax.ShapeDtypeStruct((b, tc, ds), F32),
        scratch_shapes=[pltpu.VMEM((gb, nbs * nck, w), F32),
                        pltpu.VMEM((4, nck * gb * nbs, LANE), F32),
                        pltpu.VMEM((2, nck * gb * nbs, LANE), F32),
                        pltpu.VMEM((2, nck * gb * nbs, LANE), F32)],
        compiler_params=_cparams(("parallel", "parallel")),
        name="s5",
    )(p, wpt, mt, nf, nbw, l16, d_t)


def _merge_kernel(yf_ref, yb_ref, bonus_ref, g_ref, y5_ref, x_ref, mod_ref, lnw_ref, lnb_ref, gluw_ref, glub_ref,
                  wout_ref, gffn_ref, x1_ref, h2_ref):
    ones = _head_ones()
    yh = yf_ref[...].astype(F32) + yb_ref[...].astype(F32)
    mean = _head_sum(yh, ones) * (1.0 / HEAD)
    cen = yh - mean
    var = _head_sum(cen * cen, ones) * (1.0 / HEAD)
    yn = cen * lax.rsqrt(var + LNX_EPS) * lnw_ref[...] + lnb_ref[...]
    o_rwkv = (yn + bonus_ref[...]) * g_ref[...]
    z = _gelu(y5_ref[...])
    o_s5 = z * _sigmoid(_mm(z, gluw_ref[...]) + glub_ref[...])
    o = jnp.concatenate([o_rwkv, o_s5], axis=1)
    mix = _mm(o, wout_ref[...])
    x1 = x_ref[...] + mod_ref[2:3, :] * mix
    x1_ref[...] = x1
    ms = jnp.mean(x1 * x1, axis=-1, keepdims=True)
    h = x1 * lax.rsqrt(ms + RMS_EPS) * gffn_ref[...]
    h2_ref[...] = (h * (1.0 + mod_ref[4:5, :]) + mod_ref[3:4, :]).astype(BF16)


def _merge(yf, yb, bonus, g, y5, x, mod3, lnx_w, lnx_b, glu_w, glu_b, w_out, g_ffn, n_ctx_tiles):
    b, t, d = x.shape
    dr = bonus.shape[-1]
    ds = y5.shape[-1]
    nt = t // TOK_TILE
    off = lambda bi, i: (bi, i + n_ctx_tiles, 0)
    const = lambda bi, i: (0, 0)
    return pl.pallas_call(
        _merge_kernel,
        grid=(b, nt),
        in_specs=[pl.BlockSpec((None, TOK_TILE, dr), off),
                  pl.BlockSpec((None, TOK_TILE, dr), off),
                  pl.BlockSpec((None, TOK_TILE, dr), off),
                  pl.BlockSpec((None, TOK_TILE, dr), off),
                  pl.BlockSpec((None, TOK_TILE, ds), off),
                  pl.BlockSpec((None, TOK_TILE, d), lambda bi, i: (bi, i, 0)),
                  pl.BlockSpec((None, 6, d), lambda bi, i: (bi, 0, 0)),
                  pl.BlockSpec((1, dr), const), pl.BlockSpec((1, dr), const),
                  pl.BlockSpec((ds, ds), const), pl.BlockSpec((1, ds), const),
                  pl.BlockSpec((dr + ds, d), const), pl.BlockSpec((1, d), const)],
        out_specs=[pl.BlockSpec((None, TOK_TILE, d), lambda bi, i: (bi, i, 0)),
                   pl.BlockSpec((None, TOK_TILE, d), lambda bi, i: (bi, i, 0))],
        out_shape=[jax.ShapeDtypeStruct((b, t, d), F32), jax.ShapeDtypeStruct((b, t, d), BF16)],
        compiler_params=_cparams(("parallel", "parallel")),
        name="merge",
    )(yf, yb, bonus, g, y5, x, mod3, lnx_w, lnx_b, glu_w, glu_b, w_out, g_ffn)


def _ffn_kernel(h_ref, wg_ref, wv_ref, cw_ref, cb_ref, wd_ref, x1_ref, mod_ref, gfin_ref, o_ref, acc_ref):
    f = pl.program_id(1)

    @pl.when(f == 0)
    def _():
        acc_ref[...] = jnp.zeros_like(acc_ref)

    h = h_ref[...]
    gate = jnp.dot(h, wg_ref[...], preferred_element_type=F32)
    val = jnp.dot(h, wv_ref[...], preferred_element_type=F32)
    prev, nxt = _shift_rows(gate)
    gate = cw_ref[0:1, :] * prev + cw_ref[1:2, :] * gate + cw_ref[2:3, :] * nxt + cb_ref[...]
    act = (_gelu(gate) * val).astype(BF16)
    acc_ref[...] += jnp.dot(act, wd_ref[...], preferred_element_type=F32)

    @pl.when(f == pl.num_programs(1) - 1)
    def _():
        x2 = x1_ref[...] + mod_ref[5:6, :] * acc_ref[...]
        ms = jnp.mean(x2 * x2, axis=-1, keepdims=True)
        o_ref[...] = x2 * lax.rsqrt(ms + RMS_EPS) * gfin_ref[...]


def _ffn(h2, w_up, conv_w, conv_b, w_down, x1, mod3, g_fin):
    b, t, d = x1.shape
    ff = w_down.shape[0]
    tm = min(FFN_TOK_TILE, t)
    tf = min(FFN_F_TILE, ff)
    nt, nf = t // tm, ff // tf
    return pl.pallas_call(
        _ffn_kernel,
        grid=(b * nt, nf),
        in_specs=[pl.BlockSpec((None, tm, d), lambda i, f: (i // nt, i % nt, 0)),
                  pl.BlockSpec((d, tf), lambda i, f: (0, f)),
                  pl.BlockSpec((d, tf), lambda i, f: (0, nf + f)),
                  pl.BlockSpec((3, tf), lambda i, f: (0, f)),
                  pl.BlockSpec((1, tf), lambda i, f: (0, f)),
                  pl.BlockSpec((tf, d), lambda i, f: (f, 0)),
                  pl.BlockSpec((None, tm, d), lambda i, f: (i // nt, i % nt, 0)),
                  pl.BlockSpec((None, 6, d), lambda i, f: (i // nt, 0, 0)),
                  pl.BlockSpec((1, d), lambda i, f: (0, 0))],
        out_specs=pl.BlockSpec((None, tm, d), lambda i, f: (i // nt, i % nt, 0)),
        out_shape=jax.ShapeDtypeStruct((b, t, d), F32),
        scratch_shapes=[pltpu.VMEM((tm, d), F32)],
        compiler_params=_cparams(("parallel", "arbitrary")),
        name="ffn",
    )(h2, w_up, w_up, conv_w, conv_b, w_down, x1, mod3, g_fin)


def _pad_cols(a, n):
    return jnp.pad(a, [(0, 0)] * (a.ndim - 1) + [(0, n - a.shape[-1])])


def _pack_in(a, dr, sizes, pads):
    gl, dl, al = sizes
    glp, wlp, alp = pads
    o = 3 * dr
    parts = [a[..., :o], _pad_cols(a[..., o:o + gl], glp), _pad_cols(a[..., o + gl:o + gl + 2 * dl], wlp),
             _pad_cols(a[..., o + gl + 2 * dl:o + gl + 2 * dl + 2 * al], alp), a[..., o + gl + 2 * dl + 2 * al:]]
    return jnp.concatenate(parts, axis=-1)


def _lora_up(w, rows):
    _, r, dr = w.shape
    out = jnp.zeros((rows, 2 * dr), w.dtype)
    out = out.at[0:r, 0:dr].set(w[0])
    out = out.at[r:2 * r, dr:2 * dr].set(w[1])
    return out


def kernel(x, c, ctx, c_ctx, mod_w, mod_b, norm_mix_g, w_in, shift_mu, rwkv_w0, rwkv_w_up, rwkv_a0, rwkv_a_up,
           rwkv_g_up, rwkv_k_k, rwkv_k_a, rwkv_r_k, lnx_w, lnx_b, s5_a_re, s5_a_im, s5_log_step, s5_b_re, s5_b_im,
           s5_c_re, s5_c_im, s5_d, s5_glu_w, s5_glu_b, w_out, norm_ffn_g, ffn_w_up, ffn_conv_w, ffn_conv_b,
           ffn_w_down, final_norm_g):
    assert mod_w.shape[0] == 1, "single trunk layer"
    b, t, d = x.shape
    ctx_len = ctx.shape[1]
    dr = rwkv_w0.shape[-1]
    ds = s5_d.shape[-1]
    dl, al, gl = rwkv_w_up.shape[2], rwkv_a_up.shape[2], rwkv_g_up.shape[1]
    n_groups = ds // S5_GROUP
    assert ctx_len == TOK_TILE and t % FFN_TOK_TILE == 0
    assert dr % MXU == 0 and s5_a_re.shape[-1] == S5_STATE and S5_CHUNK * S5_GROUP == MXU
    glp, wlp, alp = _round_up(gl, LANE), _round_up(2 * dl, LANE), _round_up(2 * al, LANE)
    dims = (dr, glp, wlp, alp)
    tc = ctx_len + t

    rows = _round_up(b + 1, 8)
    c_all = jnp.zeros((rows, d), F32).at[:b].set(c).at[b].set(c_ctx)
    mod3 = _mod(c_all, mod_w[0], mod_b).reshape(rows, 6, d)

    w_packed = _pack_in(w_in[0], dr, (gl, dl, al), (glp, wlp, alp)).astype(BF16)
    mu_p = _pack_in(shift_mu[0], dr, (gl, dl, al), (glp, wlp, alp))
    p = _inproj(ctx, x, mod3, norm_mix_g, w_packed, b)
    r, v, kk, lw, kd, bb, bonus, gate = _prep(
        p, mu_p, rwkv_w0[0], rwkv_a0[0], rwkv_k_k, rwkv_k_a, rwkv_r_k[0].reshape(1, dr),
        _lora_up(rwkv_w_up[0], wlp).astype(BF16), _lora_up(rwkv_a_up[0], alp).astype(BF16),
        jnp.pad(rwkv_g_up[0], [(0, glp - gl), (0, 0)]).astype(BF16), dims)

    yf, yb = _wkv(r, v, kk, lw, kd, bb, ctx_len // WKV_CHUNK)

    w = S5_CHUNK * S5_GROUP
    tile_col = lambda a: jnp.broadcast_to(a[..., None], a.shape + (w,))
    ls_t = jnp.broadcast_to(s5_log_step[0][:, :, None, None], (2, n_groups, S5_STATE, w))
    b_t = lambda a: jnp.tile(a, (1, 1, 1, S5_CHUNK))
    c_t = lambda a: jnp.tile(a.transpose(0, 1, 3, 2), (1, 1, 1, S5_CHUNK))
    row = lambda a: jnp.pad(a[:, :, None, :], [(0, 0), (0, 0), (0, 0), (0, LANE - S5_STATE)])
    ls_r = jnp.broadcast_to(s5_log_step[0][:, :, None, None], (2, n_groups, 1, LANE))
    wpt, mt, nf, nbw, l16 = _s5w(tile_col(s5_a_re[0]), tile_col(s5_a_im[0]), ls_t, b_t(s5_b_re[0]), b_t(s5_b_im[0]),
                                 c_t(s5_c_re[0]), c_t(s5_c_im[0]), s5_c_re[0], s5_c_im[0],
                                 row(s5_a_re[0]), row(s5_a_im[0]), ls_r)
    d_t = jnp.tile(s5_d[0].reshape(n_groups, 1, S5_GROUP), (1, 1, S5_CHUNK))
    y5 = _s5(p, wpt, mt, nf, nbw, l16, d_t, ds, ctx_len // S5_CHUNK)

    x1, h2 = _merge(yf, yb, bonus, gate, y5, x, mod3, lnx_w, lnx_b, s5_glu_w[0].astype(BF16), s5_glu_b,
                    w_out[0].astype(BF16), norm_ffn_g, ctx_len // TOK_TILE)
    return _ffn(h2, ffn_w_up[0].astype(BF16), ffn_conv_w[0], ffn_conv_b, ffn_w_down[0].astype(BF16), x1, mod3,
                final_norm_g.reshape(1, d))
```

```python
import functools
import math

import jax
import jax.numpy as jnp
from jax import lax
from jax.experimental import pallas as pl
from jax.experimental.pallas import tpu as pltpu

F32 = jnp.float32
BF16 = jnp.bfloat16

HEAD = 64
GRID_W = 64
S5_GROUP = 16
S5_STATE = 64
RMS_EPS = 1e-6
LNX_EPS = 64e-5
LANE = 128
MXU = 256
HEADS_PER_GROUP = MXU // HEAD
WKV_CHUNK = 64
S5_CHUNK = 16
TOK_TILE = 256
FFN_TOK_TILE = 512
FFN_F_TILE = 512
VMEM_LIMIT = 56 * 1024 * 1024


def _cparams(sem):
    return pltpu.CompilerParams(dimension_semantics=sem, vmem_limit_bytes=VMEM_LIMIT)


def _round_up(n, m):
    return (n + m - 1) // m * m


def _split(a):
    hi = a.astype(BF16)
    lo = (a - hi.astype(F32)).astype(BF16)
    return hi, lo


def _mm(a, b, dims=(((1,), (0,)), ((), ())), split_a=False, split_b=False):
    dot = functools.partial(lax.dot_general, dimension_numbers=dims, preferred_element_type=F32)
    if not split_a and not split_b:
        return dot(a.astype(BF16), b.astype(BF16))
    a_hi, a_lo = _split(a) if split_a else (a.astype(BF16), None)
    b_hi, b_lo = _split(b) if split_b else (b.astype(BF16), None)
    out = dot(a_hi, b_hi)
    if a_lo is not None:
        out = out + dot(a_lo, b_hi)
    if b_lo is not None:
        out = out + dot(a_hi, b_lo)
    return out


_NT = (((1,), (1,)), ((), ()))
_TN = (((0,), (0,)), ((), ()))


def _sigmoid(x):
    return 1.0 / (1.0 + jnp.exp(-x))


def _gelu(x):
    return 0.5 * x * (1.0 + jnp.tanh(math.sqrt(2.0 / math.pi) * (x + 0.044715 * (x * x * x))))


def _head_ones():
    r = lax.broadcasted_iota(jnp.int32, (MXU, MXU), 0) // HEAD
    c = lax.broadcasted_iota(jnp.int32, (MXU, MXU), 1) // HEAD
    return jnp.where(r == c, 1.0, 0.0).astype(BF16)


def _head_sum(x, ones):
    parts = []
    for g in range(x.shape[1] // MXU):
        parts.append(_mm(x[:, g * MXU:(g + 1) * MXU], ones, split_a=True))
    return parts[0] if len(parts) == 1 else jnp.concatenate(parts, axis=1)


def _shift_rows(x, whole_tile=False):
    n = x.shape[0]
    row = lax.broadcasted_iota(jnp.int32, (n, 1), 0)
    pos = row & (GRID_W - 1)
    first = jnp.where(whole_tile, row, pos) == 0
    last = jnp.where(whole_tile, row - (n - 1), pos - (GRID_W - 1)) == 0
    prev = jnp.where(first, 0.0, pltpu.roll(x, 1, 0))
    nxt = jnp.where(last, 0.0, pltpu.roll(x, n - 1, 0))
    return prev, nxt


def _mod_kernel(c_ref, w_ref, b_ref, o_ref):
    c = c_ref[...]
    s = c * _sigmoid(c)
    o_ref[...] = lax.dot_general(s, w_ref[...], (((1,), (0,)), ((), ())), precision=lax.Precision.HIGHEST,
                                 preferred_element_type=F32) + b_ref[...]


def _mod(c_all, mod_w, mod_b):
    rows, d = c_all.shape
    n = mod_w.shape[1]
    tn = 1024
    return pl.pallas_call(
        _mod_kernel,
        grid=(n // tn,),
        in_specs=[pl.BlockSpec((rows, d), lambda j: (0, 0)),
                  pl.BlockSpec((d, tn), lambda j: (0, j)),
                  pl.BlockSpec((1, tn), lambda j: (0, j))],
        out_specs=pl.BlockSpec((rows, tn), lambda j: (0, j)),
        out_shape=jax.ShapeDtypeStruct((rows, n), F32),
        compiler_params=_cparams(("parallel",)),
        name="mod",
    )(c_all, mod_w, mod_b)


def _front_kernel(dims, c_ref, x_ref, mod_ref, g_ref, w_ref, mu_ref, w0_ref, a0_ref, kk_w_ref, ka_ref, rk_ref,
                  wup_ref, aup_ref, gup_ref,
                  r_ref, v_ref, kk_ref, lw_ref, kd_ref, bb_ref, bonus_ref, gate_ref, u_ref):
    dr, glp, wlp, alp = dims
    is_ctx = pl.program_id(1) == 0
    x = jnp.where(is_ctx, c_ref[...], x_ref[...])
    ms = jnp.mean(x * x, axis=-1, keepdims=True)
    h = x * lax.rsqrt(ms + RMS_EPS) * g_ref[...]
    hb = (h * (1.0 + mod_ref[1:2, :]) + mod_ref[0:1, :]).astype(BF16)

    def proj(c0, c1):
        return jnp.dot(hb, w_ref[:, c0:c1], preferred_element_type=F32)

    def shifted(q, c0, c1):
        prev, nxt = _shift_rows(q, whole_tile=is_ctx)
        return q + mu_ref[0:1, c0:c1] * (prev - q) + mu_ref[1:2, c0:c1] * (nxt - q)

    o = 3 * dr
    ns = glp + wlp + alp
    ones = _head_ones()
    pr = proj(0, dr)
    pk = proj(dr, 2 * dr)
    r = shifted(pr, 0, dr)
    r_ref[...] = r.astype(r_ref.dtype)
    pv = proj(2 * dr, o)
    k = shifted(pk, dr, 2 * dr)
    kraw = k * kk_w_ref[...]
    kk = kraw / jnp.maximum(jnp.sqrt(_head_sum(kraw * kraw, ones)), 1e-12)
    kk_ref[...] = kk.astype(kk_ref.dtype)
    ps = proj(o, o + ns)
    v = shifted(pv, 2 * dr, o)
    v_ref[...] = v.astype(v_ref.dtype)
    bonus_ref[...] = (_head_sum(r * k * rk_ref[...], ones) * v).astype(bonus_ref.dtype)
    u_ref[...] = proj(o + ns, w_ref.shape[1]).astype(u_ref.dtype)
    small = shifted(ps, o, o + ns)
    gd = small[:, 0:glp]
    wd = small[:, glp:glp + wlp]
    ad = small[:, glp + wlp:ns]
    lora_w = _mm(jnp.tanh(wd), wup_ref[...], split_a=True)
    lora_a = _mm(ad, aup_ref[...], split_a=True)
    gate_ref[...] = _mm(_sigmoid(gd), gup_ref[...]).astype(gate_ref.dtype)
    for d in range(2):
        z = w0_ref[d:d + 1, :] + lora_w[:, d * dr:(d + 1) * dr]
        lw_ref[d] = -math.exp(-0.5) * _sigmoid(z)
        a = _sigmoid(a0_ref[d:d + 1, :] + lora_a[:, d * dr:(d + 1) * dr])
        kd_ref[d] = (k * (1.0 + (a - 1.0) * ka_ref[...])).astype(kd_ref.dtype)
        bb_ref[d] = (kk * a).astype(bb_ref.dtype)


def _front(ctx, x, mod3, g, w_packed, mu_p, w0, a0, k_k, k_a, r_k, wup_c, aup_c, gup_p, dims, ds, n_batch):
    b, t, d = x.shape
    npk = w_packed.shape[1]
    dr, glp, wlp, alp = dims
    nq = 3 * dr + glp + wlp + alp
    tc = ctx.shape[1] + t
    nt = tc // TOK_TILE
    tok = lambda bi, i: (bi, i, 0)
    tok2 = lambda bi, i: (0, bi, i, 0)
    const = lambda bi, i: (0, 0)
    resident = dict(pipeline_mode=pl.Buffered(1))
    one = jax.ShapeDtypeStruct((b, tc, dr), BF16)
    two = jax.ShapeDtypeStruct((2, b, tc, dr), BF16)
    lws = jax.ShapeDtypeStruct((2, b, tc, dr), F32)
    spec1 = pl.BlockSpec((None, TOK_TILE, dr), tok)
    spec2 = pl.BlockSpec((2, None, TOK_TILE, dr), tok2)
    return pl.pallas_call(
        functools.partial(_front_kernel, dims),
        grid=(b, nt),
        in_specs=[pl.BlockSpec((None, TOK_TILE, d), lambda bi, i: (bi, 0, 0), **resident),
                  pl.BlockSpec((None, TOK_TILE, d), lambda bi, i: (bi, jnp.maximum(i - 1, 0), 0)),
                  pl.BlockSpec((None, 6, d), lambda bi, i: (jnp.where(i == 0, n_batch, bi), 0, 0)),
                  pl.BlockSpec((1, d), const),
                  pl.BlockSpec((d, npk), const, **resident),
                  pl.BlockSpec((2, nq), const),
                  pl.BlockSpec((2, dr), const), pl.BlockSpec((2, dr), const),
                  pl.BlockSpec((1, dr), const), pl.BlockSpec((1, dr), const), pl.BlockSpec((1, dr), const),
                  pl.BlockSpec((wlp, 2 * dr), const, **resident), pl.BlockSpec((alp, 2 * dr), const, **resident),
                  pl.BlockSpec((glp, dr), const, **resident)],
        out_specs=[spec1, spec1, spec1, spec2, spec2, spec2, spec1, spec1, pl.BlockSpec((None, TOK_TILE, ds), tok)],
        out_shape=[one, one, one, lws, two, two, one, one, jax.ShapeDtypeStruct((b, tc, ds), F32)],
        compiler_params=_cparams(("parallel", "parallel")),
        name="front",
    )(ctx, x, mod3, g, w_packed, mu_p, w0, a0, k_k, k_a, r_k, wup_c, aup_c, gup_p)


def _bd(x, bdmask):
    return jnp.where(bdmask, jnp.concatenate([x] * HEADS_PER_GROUP, axis=0), 0.0)


def _wkv_kernel(r0_ref, v0_ref, kk0_ref, r1_ref, v1_ref, kk1_ref, lw0_ref, kd0_ref, bb0_ref, lw1_ref, kd1_ref, bb1_ref,
                y0_ref, y1_ref, s_ref):
    i = pl.program_id(1)
    L = WKV_CHUNK
    n_groups = r0_ref.shape[1] // MXU
    dirs = ((r0_ref, v0_ref, kk0_ref, lw0_ref, kd0_ref, bb0_ref, y0_ref),
            (r1_ref, v1_ref, kk1_ref, lw1_ref, kd1_ref, bb1_ref, y1_ref))

    @pl.when(i == 0)
    def _():
        s_ref[...] = jnp.zeros_like(s_ref)

    row = lax.broadcasted_iota(jnp.int32, (L, 1), 0)
    t_idx = lax.broadcasted_iota(jnp.int32, (L, MXU), 0)
    s_idx = lax.broadcasted_iota(jnp.int32, (L, MXU), 1) % HEAD
    eye_sbs = jnp.where(t_idx == s_idx, 1.0, 0.0)
    br = lax.broadcasted_iota(jnp.int32, (MXU, MXU), 0)
    bc = lax.broadcasted_iota(jnp.int32, (MXU, MXU), 1)
    bdmask = (br // HEAD) == (bc // HEAD)
    eye_bd = jnp.where(br == bc, 1.0, 0.0)

    G = range(2 * n_groups)
    sls = [slice(g * MXU, (g + 1) * MXU) for g in range(n_groups)] * 2
    outs = [dirs[0][6]] * n_groups + [dirs[1][6]] * n_groups
    kt, bt, kdt, rt, vv, p_end, strict, incl = [], [], [], [], [], [], [], []
    for d, (r_ref, v_ref, kk_ref, lw_ref, kd_ref, bb_ref, _) in enumerate(dirs):
        lw = lw_ref[...]
        pre = lw
        shift = 1
        while shift < L:
            pre = pre + jnp.where(row >= shift, pltpu.roll(pre, shift, 0), 0.0)
            shift *= 2
        cum = pre if d == 0 else pre[L - 1:L, :] - pre + lw
        for sl in sls[:n_groups]:
            c = cum[:, sl]
            p_inv = jnp.exp(-c)
            kt.append(kk_ref[:, sl].astype(F32) * jnp.exp(c - lw[:, sl]))
            bt.append(bb_ref[:, sl].astype(F32) * p_inv)
            kdt.append(kd_ref[:, sl].astype(F32) * p_inv)
            rt.append(r_ref[:, sl].astype(F32) * jnp.exp(c))
            vv.append(v_ref[:, sl].astype(F32))
            p_end.append(jnp.exp(c[L - 1:L, :] if d == 0 else c[0:1, :]))
            strict.append(t_idx > s_idx if d == 0 else t_idx < s_idx)
            incl.append(t_idx >= s_idx if d == 0 else t_idx <= s_idx)

    a_all = [_mm(jnp.concatenate([kt[g], rt[g]], axis=0),
                 jnp.concatenate([_bd(bt[g], bdmask), _bd(kdt[g], bdmask)], axis=0), _NT) for g in G]
    a_ab = [jnp.where(strict[g], a_all[g][:L, :MXU], 0.0) for g in G]
    a_ak = [jnp.where(strict[g], a_all[g][:L, MXU:], 0.0) for g in G]
    a_rb = [jnp.where(incl[g], a_all[g][L:, :MXU], 0.0) for g in G]
    a_rk = [jnp.where(incl[g], a_all[g][L:, MXU:], 0.0) for g in G]
    av = [_mm(jnp.concatenate([a_ak[g], a_rk[g]], axis=0), _bd(vv[g], bdmask)) for g in G]

    npow = [_mm(n, _bd(n, bdmask)) for n in a_ab]
    tinv = [eye_sbs - a for a in a_ab]
    for _ in range(int(math.log2(L)) - 2):
        both = [_mm(jnp.concatenate([n, t], axis=0), _bd(n, bdmask)) for n, t in zip(npow, tinv)]
        npow = [x[:L] for x in both]
        tinv = [t + x[L:] for t, x in zip(tinv, both)]
    tinv = [t + _mm(t, _bd(n, bdmask)) for t, n in zip(tinv, npow)]

    wu = [_mm(tinv[g], jnp.concatenate([_bd(kt[g], bdmask), _bd(av[g][:L], bdmask)], axis=1)) for g in G]
    wm = [w[:, :MXU] for w in wu]
    um = [w[:, MXU:] for w in wu]
    qy = [_mm(a_rb[g], jnp.concatenate([_bd(wm[g], bdmask), _bd(um[g], bdmask)], axis=1)) for g in G]
    btp = [bt[g] * p_end[g] for g in G]
    gmat = [jnp.where(bdmask, eye_bd * p_end[g] - _mm(btp[g], wm[g], _TN), 0.0) for g in G]
    hmat = [jnp.where(bdmask, _mm(jnp.concatenate([kdt[g] * p_end[g], btp[g]], axis=0),
                                  jnp.concatenate([vv[g], -um[g]], axis=0), _TN), 0.0) for g in G]
    ys = [_mm(jnp.concatenate([rt[g] - qy[g][:, :MXU], gmat[g]], axis=0), s_ref[g]) for g in G]
    for g in G:
        outs[g][:, sls[g]] = (ys[g][:L] + (av[g][L:] - qy[g][:, MXU:])).astype(outs[g].dtype)
        s_ref[g] = ys[g][L:] + hmat[g]


def _wkv(r, v, kk, lw, kd, bb, n_ctx_chunks):
    b, tc, dr = r.shape
    nch = tc // WKV_CHUNK

    def rev(i):
        return jnp.where(i < n_ctx_chunks, n_ctx_chunks - 1 - i, nch + n_ctx_chunks - 1 - i)

    one0 = pl.BlockSpec((None, WKV_CHUNK, dr), lambda bi, i: (bi, i, 0))
    one1 = pl.BlockSpec((None, WKV_CHUNK, dr), lambda bi, i: (bi, rev(i), 0))
    two0 = pl.BlockSpec((None, None, WKV_CHUNK, dr), lambda bi, i: (0, bi, i, 0))
    two1 = pl.BlockSpec((None, None, WKV_CHUNK, dr), lambda bi, i: (1, bi, rev(i), 0))
    y_shape = jax.ShapeDtypeStruct((b, tc, dr), BF16)
    return pl.pallas_call(
        _wkv_kernel,
        grid=(b, nch),
        in_specs=[one0, one0, one0, one1, one1, one1, two0, two0, two0, two1, two1, two1],
        out_specs=[one0, one1],
        out_shape=[y_shape, y_shape],
        scratch_shapes=[pltpu.VMEM((2 * dr // MXU, MXU, MXU), F32)],
        compiler_params=_cparams(("parallel", "arbitrary")),
        name="wkv",
    )(r, v, kk, r, v, kk, lw, kd, bb, lw, kd, bb)


def _cpow(lr, li, e, nbits):
    pr, pi = jnp.ones_like(lr), jnp.zeros_like(lr)
    sr, si = lr, li
    for bit in range(nbits):
        take = ((e >> bit) & 1) == 1
        pr, pi = jnp.where(take, pr * sr - pi * si, pr), jnp.where(take, pr * si + pi * sr, pi)
        if bit + 1 < nbits:
            sr, si = sr * sr - si * si, 2.0 * sr * si
    return pr, pi


def _zoh_lambda(ar, ai, ls):
    dt = jnp.exp(ls)
    mag = jnp.exp(ar * dt)
    ang = ai * dt
    return mag * jnp.cos(ang), mag * jnp.sin(ang)


def _s5w_kernel(ar_ref, ai_ref, ls_ref, br_ref, bi_ref, ctr_ref, cti_ref, cr_ref, ci_ref, arr_ref, air_ref, lsr_ref,
                wpt_ref, mt_ref, nf_ref, nb_ref, l16_ref):
    gb, p, w = ar_ref.shape[1:]
    n = S5_CHUNK
    lane_n = lax.broadcasted_iota(jnp.int32, (gb, p, w), 2) // S5_GROUP
    lane_n2 = lax.broadcasted_iota(jnp.int32, (S5_GROUP, w), 1) // S5_GROUP
    hp = lax.Precision.HIGHEST
    zpad = jnp.zeros((gb, LANE - p, w), wpt_ref.dtype)
    taps = []
    for d in range(2):
        ar, ai = ar_ref[d], ai_ref[d]
        lr, li = _zoh_lambda(ar, ai, ls_ref[d])
        den = ar * ar + ai * ai
        nr = lr - 1.0
        f_re = (nr * ar + li * ai) / den
        f_im = (li * ar - nr * ai) / den
        b_re, b_im = br_ref[d], bi_ref[d]
        bb_re = f_re * b_re - f_im * b_im
        bb_im = f_re * b_im + f_im * b_re
        pr, pi = _cpow(lr, li, (n - 1 - lane_n) if d == 0 else lane_n, 4)
        lb_re = pr * bb_re - pi * bb_im
        lb_im = pr * bb_im + pi * bb_re
        qr, qi = _cpow(lr, li, (lane_n + 1) if d == 0 else (n - lane_n), 5)
        c_re, c_im = ctr_ref[d], cti_ref[d]
        base = 2 * d * LANE
        wpt_ref[:, base:base + p, :] = lb_re.astype(wpt_ref.dtype)
        wpt_ref[:, base + p:base + LANE, :] = zpad
        wpt_ref[:, base + LANE:base + LANE + p, :] = lb_im.astype(wpt_ref.dtype)
        wpt_ref[:, base + LANE + p:base + 2 * LANE, :] = zpad
        n_ref = nf_ref if d == 0 else nb_ref
        n_ref[:, 0:p, :] = (c_re * qr - c_im * qi).astype(n_ref.dtype)
        n_ref[:, p:LANE, :] = zpad
        n_ref[:, LANE:LANE + p, :] = (-(c_re * qi + c_im * qr)).astype(n_ref.dtype)
        n_ref[:, LANE + p:2 * LANE, :] = zpad
        taps.append(jnp.einsum('ghp,gpn->ghn', cr_ref[d], lb_re, precision=hp, preferred_element_type=F32)
                    - jnp.einsum('ghp,gpn->ghn', ci_ref[d], lb_im, precision=hp, preferred_element_type=F32))
        lrr, lir = _zoh_lambda(arr_ref[d], air_ref[d], lsr_ref[d])
        for _ in range(int(math.log2(n))):
            lrr, lir = lrr * lrr - lir * lir, 2.0 * lrr * lir
        l16_ref[:, 2 * d:2 * d + 1, :] = lrr
        l16_ref[:, 2 * d + 1:2 * d + 2, :] = lir
    for g in range(gb):
        kf, kb = taps[0][g], taps[1][g]
        blocks = []
        for t in range(n):
            f = kf if t == n - 1 else pltpu.roll(kf, w - (n - 1 - t) * S5_GROUP, 1)
            r = kb if t == 0 else pltpu.roll(kb, t * S5_GROUP, 1)
            blocks.append(jnp.where(lane_n2 <= t, f, 0.0) + jnp.where(lane_n2 >= t, r, 0.0))
        mt_ref[g] = jnp.concatenate(blocks, axis=0).astype(mt_ref.dtype)


def _s5w(ar_t, ai_t, ls_t, br_t, bi_t, ctr_t, cti_t, c_re, c_im, ar_r, ai_r, ls_r):
    _, g, p, w = ar_t.shape
    gb = LANE // S5_GROUP
    big = pl.BlockSpec((2, gb, p, w), lambda j: (0, j, 0, 0))
    cs = pl.BlockSpec((2, gb, S5_GROUP, p), lambda j: (0, j, 0, 0))
    rs = pl.BlockSpec((2, gb, 1, LANE), lambda j: (0, j, 0, 0))
    tab = lambda rows: pl.BlockSpec((gb, rows, w), lambda j: (j, 0, 0))
    return pl.pallas_call(
        _s5w_kernel,
        grid=(g // gb,),
        in_specs=[big] * 7 + [cs, cs, rs, rs, rs],
        out_specs=[tab(4 * LANE), tab(w), tab(2 * LANE), tab(2 * LANE), pl.BlockSpec((gb, 4, LANE), lambda j: (j, 0, 0))],
        out_shape=[jax.ShapeDtypeStruct((g, 4 * LANE, w), BF16), jax.ShapeDtypeStruct((g, w, w), BF16),
                   jax.ShapeDtypeStruct((g, 2 * LANE, w), BF16), jax.ShapeDtypeStruct((g, 2 * LANE, w), BF16),
                   jax.ShapeDtypeStruct((g, 4, LANE), F32)],
        compiler_params=_cparams(("parallel",)),
        name="s5w",
    )(ar_t, ai_t, ls_t, br_t, bi_t, ctr_t, cti_t, c_re, c_im, ar_r, ai_r, ls_r)


def _block_transpose(xs, lane_blk):
    n = len(xs)
    rolled = []
    for dist in range(n):
        z = xs[0]
        for a in range(1, n):
            z = jnp.where(lane_blk == (a - dist) % n, xs[a], z)
        rolled.append(pltpu.roll(z, dist * S5_GROUP, 1) if dist else z)
    ys = []
    for b in range(n):
        y = rolled[(0 - b) % n]
        for a in range(1, n):
            y = jnp.where(lane_blk == a, rolled[(a - b) % n], y)
        ys.append(y)
    return ys


def _s5_kernel(n_ctx_chunks, u_ref, wpt_ref, mt_ref, nf_ref, nb_ref, l16_ref, d_ref, y_ref,
               u_scr, e_ref, sf_ref, sb_ref):
    gb = wpt_ref.shape[0]
    nbs = u_ref.shape[0]
    nck = u_ref.shape[1] // S5_CHUNK
    lanes = gb * nbs
    halves = MXU // LANE
    lane_blk = lax.broadcasted_iota(jnp.int32, (nck, LANE), 1) // S5_GROUP

    for b in range(nbs):
        for hf in range(halves):
            xs = [u_ref[b, pl.ds(hf * gb + s8, nck, stride=S5_CHUNK), :] for s8 in range(gb)]
            for g, ug in enumerate(_block_transpose(xs, lane_blk)):
                u_scr[g, b * nck:(b + 1) * nck, hf * LANE:(hf + 1) * LANE] = ug

    for g in range(gb):
        e = _mm(u_scr[g], wpt_ref[g], _NT)
        for b in range(nbs):
            for k in range(4):
                e_ref[k, pl.ds(g * nbs + b, nck, stride=lanes), :] = e[b * nck:(b + 1) * nck, k * LANE:(k + 1) * LANE]

    def lam_rows(k):
        return jnp.concatenate([jnp.broadcast_to(l16_ref[g, k:k + 1, :], (nbs, LANE)) for g in range(gb)], axis=0)

    lfr, lfi, lbr, lbi = (lam_rows(k) for k in range(4))
    zero = jnp.zeros((lanes, LANE), F32)

    def step(k, carry):
        fr, fi, br, bi = carry
        cb = jnp.where(k < n_ctx_chunks, n_ctx_chunks - 1 - k, nck - 1 + n_ctx_chunks - k)
        of = pl.multiple_of(k * lanes, lanes)
        ob = pl.multiple_of(cb * lanes, lanes)
        sf_ref[0, pl.ds(of, lanes), :] = fr
        sf_ref[1, pl.ds(of, lanes), :] = fi
        sb_ref[0, pl.ds(ob, lanes), :] = br
        sb_ref[1, pl.ds(ob, lanes), :] = bi
        efr, efi = e_ref[0, pl.ds(of, lanes), :], e_ref[1, pl.ds(of, lanes), :]
        ebr, ebi = e_ref[2, pl.ds(ob, lanes), :], e_ref[3, pl.ds(ob, lanes), :]
        return (lfr * fr - lfi * fi + efr, lfr * fi + lfi * fr + efi,
                lbr * br - lbi * bi + ebr, lbr * bi + lbi * br + ebi)

    lax.fori_loop(0, nck, step, (zero, zero, zero, zero))

    def rows_of(s_ref, g):
        return jnp.concatenate(
            [jnp.concatenate([s_ref[k, pl.ds(g * nbs + b, nck, stride=lanes), :] for k in range(2)], axis=1)
             for b in range(nbs)], axis=0)

    for g in range(gb):
        u = u_scr[g]
        y = _mm(u, mt_ref[g], _NT) + _mm(rows_of(sf_ref, g), nf_ref[g]) + _mm(rows_of(sb_ref, g), nb_ref[g])
        u_scr[g] = y + u * d_ref[g]

    for b in range(nbs):
        for hf in range(halves):
            ys = [u_scr[g, b * nck:(b + 1) * nck, hf * LANE:(hf + 1) * LANE] for g in range(gb)]
            for t8, yt in enumerate(_block_transpose(ys, lane_blk)):
                y_ref[b, pl.ds(hf * gb + t8, nck, stride=S5_CHUNK), :] = yt


def _s5(u, wpt, mt, nf, nbw, l16, d_t, n_ctx_chunks):
    b, tc, ds = u.shape
    g, _, w = mt.shape
    gb = LANE // S5_GROUP
    nbs = 2 if b % 2 == 0 else 1
    nck = tc // S5_CHUNK
    grp = lambda shape: pl.BlockSpec((gb,) + shape, lambda j, bh: (j,) + (0,) * len(shape))
    return pl.pallas_call(
        functools.partial(_s5_kernel, n_ctx_chunks),
        grid=(g // gb, b // nbs),
        in_specs=[pl.BlockSpec((nbs, tc, LANE), lambda j, bh: (bh, 0, j)),
                  grp((4 * LANE, w)), grp((w, w)), grp((2 * LANE, w)), grp((2 * LANE, w)), grp((4, LANE)),
                  grp((1, w))],
        out_specs=pl.BlockSpec((nbs, tc, LANE), lambda j, bh: (bh, 0, j)),
        out_shape=jax.ShapeDtypeStruct((b, tc, ds), F32),
        scratch_shapes=[pltpu.VMEM((gb, nbs * nck, w), F32),
                        pltpu.VMEM((4, nck * gb * nbs, LANE), F32),
                        pltpu.VMEM((2, nck * gb * nbs, LANE), F32),
                        pltpu.VMEM((2, nck * gb * nbs, LANE), F32)],
        compiler_params=_cparams(("parallel", "parallel")),
        name="s5",
    )(u, wpt, mt, nf, nbw, l16, d_t)


def _merge_kernel(yf_ref, yb_ref, bonus_ref, g_ref, y5_ref, x_ref, mod_ref, lnw_ref, lnb_ref, gluw_ref, glub_ref,
                  wout_ref, gffn_ref, x1_ref, h2_ref):
    ones = _head_ones()
    yh = yf_ref[...].astype(F32) + yb_ref[...].astype(F32)
    mean = _head_sum(yh, ones) * (1.0 / HEAD)
    cen = yh - mean
    var = _head_sum(cen * cen, ones) * (1.0 / HEAD)
    yn = cen * lax.rsqrt(var + LNX_EPS) * lnw_ref[...] + lnb_ref[...]
    o_rwkv = (yn + bonus_ref[...]) * g_ref[...]
    z = _gelu(y5_ref[...])
    o_s5 = z * _sigmoid(_mm(z, gluw_ref[...]) + glub_ref[...])
    o = jnp.concatenate([o_rwkv, o_s5], axis=1)
    mix = _mm(o, wout_ref[...])
    x1 = x_ref[...] + mod_ref[2:3, :] * mix
    x1_ref[...] = x1
    ms = jnp.mean(x1 * x1, axis=-1, keepdims=True)
    h = x1 * lax.rsqrt(ms + RMS_EPS) * gffn_ref[...]
    h2_ref[...] = (h * (1.0 + mod_ref[4:5, :]) + mod_ref[3:4, :]).astype(BF16)


def _merge(yf, yb, bonus, g, y5, x, mod3, lnx_w, lnx_b, glu_w, glu_b, w_out, g_ffn, n_ctx_tiles):
    b, t, d = x.shape
    dr = bonus.shape[-1]
    ds = y5.shape[-1]
    nt = t // TOK_TILE
    off = lambda bi, i: (bi, i + n_ctx_tiles, 0)
    const = lambda bi, i: (0, 0)
    return pl.pallas_call(
        _merge_kernel,
        grid=(b, nt),
        in_specs=[pl.BlockSpec((None, TOK_TILE, dr), off),
                  pl.BlockSpec((None, TOK_TILE, dr), off),
                  pl.BlockSpec((None, TOK_TILE, dr), off),
                  pl.BlockSpec((None, TOK_TILE, dr), off),
                  pl.BlockSpec((None, TOK_TILE, ds), off),
                  pl.BlockSpec((None, TOK_TILE, d), lambda bi, i: (bi, i, 0)),
                  pl.BlockSpec((None, 6, d), lambda bi, i: (bi, 0, 0)),
                  pl.BlockSpec((1, dr), const), pl.BlockSpec((1, dr), const),
                  pl.BlockSpec((ds, ds), const), pl.BlockSpec((1, ds), const),
                  pl.BlockSpec((dr + ds, d), const), pl.BlockSpec((1, d), const)],
        out_specs=[pl.BlockSpec((None, TOK_TILE, d), lambda bi, i: (bi, i, 0)),
                   pl.BlockSpec((None, TOK_TILE, d), lambda bi, i: (bi, i, 0))],
        out_shape=[jax.ShapeDtypeStruct((b, t, d), F32), jax.ShapeDtypeStruct((b, t, d), BF16)],
        compiler_params=_cparams(("parallel", "parallel")),
        name="merge",
    )(yf, yb, bonus, g, y5, x, mod3, lnx_w, lnx_b, glu_w, glu_b, w_out, g_ffn)


def _ffn_kernel(h_ref, wg_ref, wv_ref, cw_ref, cb_ref, wd_ref, x1_ref, mod_ref, gfin_ref, o_ref, acc_ref):
    f = pl.program_id(1)

    @pl.when(f == 0)
    def _():
        acc_ref[...] = jnp.zeros_like(acc_ref)

    h = h_ref[...]
    gate = jnp.dot(h, wg_ref[...], preferred_element_type=F32)
    val = jnp.dot(h, wv_ref[...], preferred_element_type=F32)
    prev, nxt = _shift_rows(gate)
    gate = cw_ref[0:1, :] * prev + cw_ref[1:2, :] * gate + cw_ref[2:3, :] * nxt + cb_ref[...]
    act = (_gelu(gate) * val).astype(BF16)
    acc_ref[...] += jnp.dot(act, wd_ref[...], preferred_element_type=F32)

    @pl.when(f == pl.num_programs(1) - 1)
    def _():
        x2 = x1_ref[...] + mod_ref[5:6, :] * acc_ref[...]
        ms = jnp.mean(x2 * x2, axis=-1, keepdims=True)
        o_ref[...] = x2 * lax.rsqrt(ms + RMS_EPS) * gfin_ref[...]


def _ffn(h2, w_up, conv_w, conv_b, w_down, x1, mod3, g_fin):
    b, t, d = x1.shape
    ff = w_down.shape[0]
    tm = min(FFN_TOK_TILE, t)
    tf = min(FFN_F_TILE, ff)
    nt, nf = t // tm, ff // tf
    return pl.pallas_call(
        _ffn_kernel,
        grid=(b * nt, nf),
        in_specs=[pl.BlockSpec((None, tm, d), lambda i, f: (i // nt, i % nt, 0)),
                  pl.BlockSpec((d, tf), lambda i, f: (0, f)),
                  pl.BlockSpec((d, tf), lambda i, f: (0, nf + f)),
                  pl.BlockSpec((3, tf), lambda i, f: (0, f)),
                  pl.BlockSpec((1, tf), lambda i, f: (0, f)),
                  pl.BlockSpec((tf, d), lambda i, f: (f, 0)),
                  pl.BlockSpec((None, tm, d), lambda i, f: (i // nt, i % nt, 0)),
                  pl.BlockSpec((None, 6, d), lambda i, f: (i // nt, 0, 0)),
                  pl.BlockSpec((1, d), lambda i, f: (0, 0))],
        out_specs=pl.BlockSpec((None, tm, d), lambda i, f: (i // nt, i % nt, 0)),
        out_shape=jax.ShapeDtypeStruct((b, t, d), F32),
        scratch_shapes=[pltpu.VMEM((tm, d), F32)],
        compiler_params=_cparams(("parallel", "arbitrary")),
        name="ffn",
    )(h2, w_up, w_up, conv_w, conv_b, w_down, x1, mod3, g_fin)


def _pad_cols(a, n):
    return jnp.pad(a, [(0, 0)] * (a.ndim - 1) + [(0, n - a.shape[-1])])


def _pack_in(a, dr, sizes, pads):
    gl, dl, al = sizes
    glp, wlp, alp = pads
    o = 3 * dr
    parts = [a[..., :o], _pad_cols(a[..., o:o + gl], glp), _pad_cols(a[..., o + gl:o + gl + 2 * dl], wlp),
             _pad_cols(a[..., o + gl + 2 * dl:o + gl + 2 * dl + 2 * al], alp), a[..., o + gl + 2 * dl + 2 * al:]]
    return jnp.concatenate(parts, axis=-1)


def _lora_up(w, rows):
    _, r, dr = w.shape
    out = jnp.zeros((rows, 2 * dr), w.dtype)
    out = out.at[0:r, 0:dr].set(w[0])
    out = out.at[r:2 * r, dr:2 * dr].set(w[1])
    return out


def kernel(x, c, ctx, c_ctx, mod_w, mod_b, norm_mix_g, w_in, shift_mu, rwkv_w0, rwkv_w_up, rwkv_a0, rwkv_a_up,
           rwkv_g_up, rwkv_k_k, rwkv_k_a, rwkv_r_k, lnx_w, lnx_b, s5_a_re, s5_a_im, s5_log_step, s5_b_re, s5_b_im,
           s5_c_re, s5_c_im, s5_d, s5_glu_w, s5_glu_b, w_out, norm_ffn_g, ffn_w_up, ffn_conv_w, ffn_conv_b,
           ffn_w_down, final_norm_g):
    assert mod_w.shape[0] == 1, "single trunk layer"
    b, t, d = x.shape
    ctx_len = ctx.shape[1]
    dr = rwkv_w0.shape[-1]
    ds = s5_d.shape[-1]
    dl, al, gl = rwkv_w_up.shape[2], rwkv_a_up.shape[2], rwkv_g_up.shape[1]
    n_groups = ds // S5_GROUP
    assert ctx_len == TOK_TILE and t % FFN_TOK_TILE == 0
    assert dr % MXU == 0 and s5_a_re.shape[-1] == S5_STATE and S5_CHUNK * S5_GROUP == MXU
    glp, wlp, alp = _round_up(gl, LANE), _round_up(2 * dl, LANE), _round_up(2 * al, LANE)
    dims = (dr, glp, wlp, alp)
    tc = ctx_len + t

    rows = _round_up(b + 1, 8)
    c_all = jnp.zeros((rows, d), F32).at[:b].set(c).at[b].set(c_ctx)
    mod3 = _mod(c_all, mod_w[0], mod_b).reshape(rows, 6, d)

    w_packed = _pack_in(w_in[0], dr, (gl, dl, al), (glp, wlp, alp)).astype(BF16)
    mu_p = _pack_in(shift_mu[0], dr, (gl, dl, al), (glp, wlp, alp))
    r, v, kk, lw, kd, bb, bonus, gate, u = _front(
        ctx, x, mod3, norm_mix_g, w_packed, mu_p, rwkv_w0[0], rwkv_a0[0], rwkv_k_k, rwkv_k_a,
        rwkv_r_k[0].reshape(1, dr), _lora_up(rwkv_w_up[0], wlp).astype(BF16),
        _lora_up(rwkv_a_up[0], alp).astype(BF16), jnp.pad(rwkv_g_up[0], [(0, glp - gl), (0, 0)]).astype(BF16),
        dims, ds, b)

    yf, yb = _wkv(r, v, kk, lw, kd, bb, ctx_len // WKV_CHUNK)

    w = S5_CHUNK * S5_GROUP
    tile_col = lambda a: jnp.broadcast_to(a[..., None], a.shape + (w,))
    ls_t = jnp.broadcast_to(s5_log_step[0][:, :, None, None], (2, n_groups, S5_STATE, w))
    b_t = lambda a: jnp.tile(a, (1, 1, 1, S5_CHUNK))
    c_t = lambda a: jnp.tile(a.transpose(0, 1, 3, 2), (1, 1, 1, S5_CHUNK))
    row = lambda a: jnp.pad(a[:, :, None, :], [(0, 0), (0, 0), (0, 0), (0, LANE - S5_STATE)])
    ls_r = jnp.broadcast_to(s5_log_step[0][:, :, None, None], (2, n_groups, 1, LANE))
    wpt, mt, nf, nbw, l16 = _s5w(tile_col(s5_a_re[0]), tile_col(s5_a_im[0]), ls_t, b_t(s5_b_re[0]), b_t(s5_b_im[0]),
                                 c_t(s5_c_re[0]), c_t(s5_c_im[0]), s5_c_re[0], s5_c_im[0],
                                 row(s5_a_re[0]), row(s5_a_im[0]), ls_r)
    d_t = jnp.tile(s5_d[0].reshape(n_groups, 1, S5_GROUP), (1, 1, S5_CHUNK))
    y5 = _s5(u, wpt, mt, nf, nbw, l16, d_t, ctx_len // S5_CHUNK)

    x1, h2 = _merge(yf, yb, bonus, gate, y5, x, mod3, lnx_w, lnx_b, s5_glu_w[0].astype(BF16), s5_glu_b,
                    w_out[0].astype(BF16), norm_ffn_g, ctx_len // TOK_TILE)
    return _ffn(h2, ffn_w_up[0].astype(BF16), ffn_conv_w[0], ffn_conv_b, ffn_w_down[0].astype(BF16), x1, mod3,
                final_norm_g.reshape(1, d))
```

```python
import functools
import math

import jax
import jax.numpy as jnp
from jax import lax
from jax.experimental import pallas as pl
from jax.experimental.pallas import tpu as pltpu

F32 = jnp.float32
BF16 = jnp.bfloat16

HEAD = 64
GRID_W = 64
S5_GROUP = 16
S5_STATE = 64
RMS_EPS = 1e-6
LNX_EPS = 64e-5
LANE = 128
MXU = 256
HEADS_PER_GROUP = MXU // HEAD
WKV_CHUNK = 64
WKV_STEP_CHUNKS = 2
S5_CHUNK = 16
TOK_TILE = 256
FFN_TOK_TILE = 512
FFN_F_TILE = 512
VMEM_LIMIT = 56 * 1024 * 1024


def _cparams(sem):
    return pltpu.CompilerParams(dimension_semantics=sem, vmem_limit_bytes=VMEM_LIMIT)


def _round_up(n, m):
    return (n + m - 1) // m * m


def _split(a):
    hi = a.astype(BF16)
    lo = (a - hi.astype(F32)).astype(BF16)
    return hi, lo


def _mm(a, b, dims=(((1,), (0,)), ((), ())), split_a=False, split_b=False):
    dot = functools.partial(lax.dot_general, dimension_numbers=dims, preferred_element_type=F32)
    if not split_a and not split_b:
        return dot(a.astype(BF16), b.astype(BF16))
    a_hi, a_lo = _split(a) if split_a else (a.astype(BF16), None)
    b_hi, b_lo = _split(b) if split_b else (b.astype(BF16), None)
    out = dot(a_hi, b_hi)
    if a_lo is not None:
        out = out + dot(a_lo, b_hi)
    if b_lo is not None:
        out = out + dot(a_hi, b_lo)
    return out


_NT = (((1,), (1,)), ((), ()))
_TN = (((0,), (0,)), ((), ()))


def _sigmoid(x):
    return 1.0 / (1.0 + jnp.exp(-x))


def _gelu(x):
    return 0.5 * x * (1.0 + jnp.tanh(math.sqrt(2.0 / math.pi) * (x + 0.044715 * (x * x * x))))


def _head_ones():
    r = lax.broadcasted_iota(jnp.int32, (MXU, MXU), 0) // HEAD
    c = lax.broadcasted_iota(jnp.int32, (MXU, MXU), 1) // HEAD
    return jnp.where(r == c, 1.0, 0.0).astype(BF16)


def _head_sum(x, ones):
    parts = []
    for g in range(x.shape[1] // MXU):
        parts.append(_mm(x[:, g * MXU:(g + 1) * MXU], ones, split_a=True))
    return parts[0] if len(parts) == 1 else jnp.concatenate(parts, axis=1)


def _shift_rows(x, whole_tile=False):
    n = x.shape[0]
    row = lax.broadcasted_iota(jnp.int32, (n, 1), 0)
    pos = row & (GRID_W - 1)
    first = jnp.where(whole_tile, row, pos) == 0
    last = jnp.where(whole_tile, row - (n - 1), pos - (GRID_W - 1)) == 0
    prev = jnp.where(first, 0.0, pltpu.roll(x, 1, 0))
    nxt = jnp.where(last, 0.0, pltpu.roll(x, n - 1, 0))
    return prev, nxt


def _mod_kernel(c_ref, w_ref, b_ref, o_ref):
    c = c_ref[...]
    s = c * _sigmoid(c)
    o_ref[...] = lax.dot_general(s, w_ref[...], (((1,), (0,)), ((), ())), precision=lax.Precision.HIGHEST,
                                 preferred_element_type=F32) + b_ref[...]


def _mod(c_all, mod_w, mod_b):
    rows, d = c_all.shape
    n = mod_w.shape[1]
    tn = 1024
    return pl.pallas_call(
        _mod_kernel,
        grid=(n // tn,),
        in_specs=[pl.BlockSpec((rows, d), lambda j: (0, 0)),
                  pl.BlockSpec((d, tn), lambda j: (0, j)),
                  pl.BlockSpec((1, tn), lambda j: (0, j))],
        out_specs=pl.BlockSpec((rows, tn), lambda j: (0, j)),
        out_shape=jax.ShapeDtypeStruct((rows, n), F32),
        compiler_params=_cparams(("parallel",)),
        name="mod",
    )(c_all, mod_w, mod_b)


def _front_kernel(dims, c_ref, x_ref, mod_ref, g_ref, w_ref, mu_ref, w0_ref, a0_ref, kk_w_ref, ka_ref, rk_ref,
                  wup_ref, aup_ref, gup_ref,
                  r_ref, v_ref, kk_ref, lw_ref, kd_ref, bb_ref, bonus_ref, gate_ref, u_ref):
    dr, glp, wlp, alp = dims
    is_ctx = pl.program_id(1) == 0
    x = jnp.where(is_ctx, c_ref[...], x_ref[...])
    ms = jnp.mean(x * x, axis=-1, keepdims=True)
    h = x * lax.rsqrt(ms + RMS_EPS) * g_ref[...]
    hb = (h * (1.0 + mod_ref[1:2, :]) + mod_ref[0:1, :]).astype(BF16)

    def proj(c0, c1):
        return jnp.dot(hb, w_ref[:, c0:c1], preferred_element_type=F32)

    def shifted(q, c0, c1):
        prev, nxt = _shift_rows(q, whole_tile=is_ctx)
        return q + mu_ref[0:1, c0:c1] * (prev - q) + mu_ref[1:2, c0:c1] * (nxt - q)

    o = 3 * dr
    ns = glp + wlp + alp
    ones = _head_ones()
    pr = proj(0, dr)
    pk = proj(dr, 2 * dr)
    r = shifted(pr, 0, dr)
    r_ref[...] = r.astype(r_ref.dtype)
    pv = proj(2 * dr, o)
    k = shifted(pk, dr, 2 * dr)
    kraw = k * kk_w_ref[...]
    kk = kraw / jnp.maximum(jnp.sqrt(_head_sum(kraw * kraw, ones)), 1e-12)
    kk_ref[...] = kk.astype(kk_ref.dtype)
    ps = proj(o, o + ns)
    v = shifted(pv, 2 * dr, o)
    v_ref[...] = v.astype(v_ref.dtype)
    bonus_ref[...] = (_head_sum(r * k * rk_ref[...], ones) * v).astype(bonus_ref.dtype)
    u_ref[...] = proj(o + ns, w_ref.shape[1]).astype(u_ref.dtype)
    small = shifted(ps, o, o + ns)
    gd = small[:, 0:glp]
    wd = small[:, glp:glp + wlp]
    ad = small[:, glp + wlp:ns]
    lora_w = _mm(jnp.tanh(wd), wup_ref[...], split_a=True)
    lora_a = _mm(ad, aup_ref[...], split_a=True)
    gate_ref[...] = _mm(_sigmoid(gd), gup_ref[...]).astype(gate_ref.dtype)
    for d in range(2):
        z = w0_ref[d:d + 1, :] + lora_w[:, d * dr:(d + 1) * dr]
        lw_ref[d] = -math.exp(-0.5) * _sigmoid(z)
        a = _sigmoid(a0_ref[d:d + 1, :] + lora_a[:, d * dr:(d + 1) * dr])
        kd_ref[d] = (k * (1.0 + (a - 1.0) * ka_ref[...])).astype(kd_ref.dtype)
        bb_ref[d] = (kk * a).astype(bb_ref.dtype)


def _front(ctx, x, mod3, g, w_packed, mu_p, w0, a0, k_k, k_a, r_k, wup_c, aup_c, gup_p, dims, ds, n_batch):
    b, t, d = x.shape
    npk = w_packed.shape[1]
    dr, glp, wlp, alp = dims
    nq = 3 * dr + glp + wlp + alp
    tc = ctx.shape[1] + t
    nt = tc // TOK_TILE
    tok = lambda bi, i: (bi, i, 0)
    tok2 = lambda bi, i: (0, bi, i, 0)
    const = lambda bi, i: (0, 0)
    resident = dict(pipeline_mode=pl.Buffered(1))
    one = jax.ShapeDtypeStruct((b, tc, dr), BF16)
    two = jax.ShapeDtypeStruct((2, b, tc, dr), BF16)
    lws = jax.ShapeDtypeStruct((2, b, tc, dr), F32)
    spec1 = pl.BlockSpec((None, TOK_TILE, dr), tok)
    spec2 = pl.BlockSpec((2, None, TOK_TILE, dr), tok2)
    return pl.pallas_call(
        functools.partial(_front_kernel, dims),
        grid=(b, nt),
        in_specs=[pl.BlockSpec((None, TOK_TILE, d), lambda bi, i: (bi, 0, 0), **resident),
                  pl.BlockSpec((None, TOK_TILE, d), lambda bi, i: (bi, jnp.maximum(i - 1, 0), 0)),
                  pl.BlockSpec((None, 6, d), lambda bi, i: (jnp.where(i == 0, n_batch, bi), 0, 0)),
                  pl.BlockSpec((1, d), const),
                  pl.BlockSpec((d, npk), const, **resident),
                  pl.BlockSpec((2, nq), const),
                  pl.BlockSpec((2, dr), const), pl.BlockSpec((2, dr), const),
                  pl.BlockSpec((1, dr), const), pl.BlockSpec((1, dr), const), pl.BlockSpec((1, dr), const),
                  pl.BlockSpec((wlp, 2 * dr), const, **resident), pl.BlockSpec((alp, 2 * dr), const, **resident),
                  pl.BlockSpec((glp, dr), const, **resident)],
        out_specs=[spec1, spec1, spec1, spec2, spec2, spec2, spec1, spec1, pl.BlockSpec((None, TOK_TILE, ds), tok)],
        out_shape=[one, one, one, lws, two, two, one, one, jax.ShapeDtypeStruct((b, tc, ds), F32)],
        compiler_params=_cparams(("parallel", "parallel")),
        name="front",
    )(ctx, x, mod3, g, w_packed, mu_p, w0, a0, k_k, k_a, r_k, wup_c, aup_c, gup_p)


def _bd(x, bdmask):
    return jnp.where(bdmask, jnp.concatenate([x] * HEADS_PER_GROUP, axis=0), 0.0)


def _wkv_kernel(r0_ref, v0_ref, kk0_ref, r1_ref, v1_ref, kk1_ref, lw0_ref, kd0_ref, bb0_ref, lw1_ref, kd1_ref, bb1_ref,
                y0_ref, y1_ref, s_ref):
    i = pl.program_id(1)
    L = WKV_CHUNK
    n_groups = r0_ref.shape[1] // MXU
    dirs = ((r0_ref, v0_ref, kk0_ref, lw0_ref, kd0_ref, bb0_ref, y0_ref),
            (r1_ref, v1_ref, kk1_ref, lw1_ref, kd1_ref, bb1_ref, y1_ref))

    @pl.when(i == 0)
    def _():
        s_ref[...] = jnp.zeros_like(s_ref)

    row = lax.broadcasted_iota(jnp.int32, (L, 1), 0)
    t_idx = lax.broadcasted_iota(jnp.int32, (L, MXU), 0)
    s_idx = lax.broadcasted_iota(jnp.int32, (L, MXU), 1) % HEAD
    eye_sbs = jnp.where(t_idx == s_idx, 1.0, 0.0)
    br = lax.broadcasted_iota(jnp.int32, (MXU, MXU), 0)
    bc = lax.broadcasted_iota(jnp.int32, (MXU, MXU), 1)
    bdmask = (br // HEAD) == (bc // HEAD)
    eye_bd = jnp.where(br == bc, 1.0, 0.0)

    kt, bt, kdt, rt, vv, p_end, strict, incl, where = [], [], [], [], [], [], [], [], []
    for d, (r_ref, v_ref, kk_ref, lw_ref, kd_ref, bb_ref, y_ref) in enumerate(dirs):
        for slot in range(WKV_STEP_CHUNKS):
            r0 = (slot if d == 0 else WKV_STEP_CHUNKS - 1 - slot) * L
            rows = slice(r0, r0 + L)
            lw = lw_ref[rows, :]
            pre = lw
            shift = 1
            while shift < L:
                pre = pre + jnp.where(row >= shift, pltpu.roll(pre, shift, 0), 0.0)
                shift *= 2
            cum = pre if d == 0 else pre[L - 1:L, :] - pre + lw
            for g in range(n_groups):
                sl = slice(g * MXU, (g + 1) * MXU)
                c = cum[:, sl]
                p_inv = jnp.exp(-c)
                kt.append(kk_ref[rows, sl].astype(F32) * jnp.exp(c - lw[:, sl]))
                bt.append(bb_ref[rows, sl].astype(F32) * p_inv)
                kdt.append(kd_ref[rows, sl].astype(F32) * p_inv)
                rt.append(r_ref[rows, sl].astype(F32) * jnp.exp(c))
                vv.append(v_ref[rows, sl].astype(F32))
                p_end.append(jnp.exp(c[L - 1:L, :] if d == 0 else c[0:1, :]))
                strict.append(t_idx > s_idx if d == 0 else t_idx < s_idx)
                incl.append(t_idx >= s_idx if d == 0 else t_idx <= s_idx)
                where.append((slot, d * n_groups + g, y_ref, rows, sl))
    G = range(len(kt))

    a_all = [_mm(jnp.concatenate([kt[g], rt[g]], axis=0),
                 jnp.concatenate([_bd(bt[g], bdmask), _bd(kdt[g], bdmask)], axis=0), _NT) for g in G]
    a_ab = [jnp.where(strict[g], a_all[g][:L, :MXU], 0.0) for g in G]
    a_ak = [jnp.where(strict[g], a_all[g][:L, MXU:], 0.0) for g in G]
    a_rb = [jnp.where(incl[g], a_all[g][L:, :MXU], 0.0) for g in G]
    a_rk = [jnp.where(incl[g], a_all[g][L:, MXU:], 0.0) for g in G]
    av = [_mm(jnp.concatenate([a_ak[g], a_rk[g]], axis=0), _bd(vv[g], bdmask)) for g in G]

    npow = [_mm(n, _bd(n, bdmask)) for n in a_ab]
    tinv = [eye_sbs - a for a in a_ab]
    for _ in range(int(math.log2(L)) - 2):
        both = [_mm(jnp.concatenate([n, t], axis=0), _bd(n, bdmask)) for n, t in zip(npow, tinv)]
        npow = [x[:L] for x in both]
        tinv = [t + x[L:] for t, x in zip(tinv, both)]
    tinv = [t + _mm(t, _bd(n, bdmask)) for t, n in zip(tinv, npow)]

    wu = [_mm(tinv[g], jnp.concatenate([_bd(kt[g], bdmask), _bd(av[g][:L], bdmask)], axis=1)) for g in G]
    wm = [w[:, :MXU] for w in wu]
    um = [w[:, MXU:] for w in wu]
    qy = [_mm(a_rb[g], jnp.concatenate([_bd(wm[g], bdmask), _bd(um[g], bdmask)], axis=1)) for g in G]
    btp = [bt[g] * p_end[g] for g in G]
    gmat = [jnp.where(bdmask, eye_bd * p_end[g] - _mm(btp[g], wm[g], _TN), 0.0) for g in G]
    hmat = [jnp.where(bdmask, _mm(jnp.concatenate([kdt[g] * p_end[g], btp[g]], axis=0),
                                  jnp.concatenate([vv[g], -um[g]], axis=0), _TN), 0.0) for g in G]
    state = [s_ref[j] for j in range(2 * n_groups)]
    for slot in range(WKV_STEP_CHUNKS):
        now = [g for g in G if where[g][0] == slot]
        ys = {g: _mm(jnp.concatenate([rt[g] - qy[g][:, :MXU], gmat[g]], axis=0), state[where[g][1]]) for g in now}
        for g in now:
            _, j, y_ref, rows, sl = where[g]
            y_ref[rows, sl] = (ys[g][:L] + (av[g][L:] - qy[g][:, MXU:])).astype(y_ref.dtype)
            state[j] = ys[g][L:] + hmat[g]
    for j in range(2 * n_groups):
        s_ref[j] = state[j]


def _wkv(r, v, kk, lw, kd, bb, n_ctx_chunks):
    b, tc, dr = r.shape
    rows = WKV_STEP_CHUNKS * WKV_CHUNK
    nblk = tc // rows
    n_ctx = n_ctx_chunks // WKV_STEP_CHUNKS
    assert tc % rows == 0 and n_ctx_chunks % WKV_STEP_CHUNKS == 0

    def rev(i):
        return jnp.where(i < n_ctx, n_ctx - 1 - i, nblk + n_ctx - 1 - i)

    one0 = pl.BlockSpec((None, rows, dr), lambda bi, i: (bi, i, 0))
    one1 = pl.BlockSpec((None, rows, dr), lambda bi, i: (bi, rev(i), 0))
    two0 = pl.BlockSpec((None, None, rows, dr), lambda bi, i: (0, bi, i, 0))
    two1 = pl.BlockSpec((None, None, rows, dr), lambda bi, i: (1, bi, rev(i), 0))
    y_shape = jax.ShapeDtypeStruct((b, tc, dr), BF16)
    return pl.pallas_call(
        _wkv_kernel,
        grid=(b, nblk),
        in_specs=[one0, one0, one0, one1, one1, one1, two0, two0, two0, two1, two1, two1],
        out_specs=[one0, one1],
        out_shape=[y_shape, y_shape],
        scratch_shapes=[pltpu.VMEM((2 * dr // MXU, MXU, MXU), F32)],
        compiler_params=_cparams(("parallel", "arbitrary")),
        name="wkv",
    )(r, v, kk, r, v, kk, lw, kd, bb, lw, kd, bb)


def _cpow(lr, li, e, nbits):
    pr, pi = jnp.ones_like(lr), jnp.zeros_like(lr)
    sr, si = lr, li
    for bit in range(nbits):
        take = ((e >> bit) & 1) == 1
        pr, pi = jnp.where(take, pr * sr - pi * si, pr), jnp.where(take, pr * si + pi * sr, pi)
        if bit + 1 < nbits:
            sr, si = sr * sr - si * si, 2.0 * sr * si
    return pr, pi


def _zoh_lambda(ar, ai, ls):
    dt = jnp.exp(ls)
    mag = jnp.exp(ar * dt)
    ang = ai * dt
    return mag * jnp.cos(ang), mag * jnp.sin(ang)


def _s5w_kernel(ar_ref, ai_ref, ls_ref, br_ref, bi_ref, ctr_ref, cti_ref, cr_ref, ci_ref,
                wpt_ref, mt_ref, nf_ref, nb_ref, l16_ref):
    gb, p, w = br_ref.shape[1:]
    n = S5_CHUNK
    lane_n = lax.broadcasted_iota(jnp.int32, (gb, p, w), 2) // S5_GROUP
    lane_n2 = lax.broadcasted_iota(jnp.int32, (S5_GROUP, w), 1) // S5_GROUP
    hp = lax.Precision.HIGHEST
    zpad = jnp.zeros((gb, LANE - p, w), wpt_ref.dtype)

    def cols(q):
        qt = jnp.concatenate([q, jnp.zeros((LANE - gb, LANE), F32)], axis=0).T
        return jnp.stack([jnp.broadcast_to(qt[0:p, g:g + 1], (p, w)) for g in range(gb)], axis=0)

    taps = []
    for d in range(2):
        ar, ai = ar_ref[d], ai_ref[d]
        lrr, lir = _zoh_lambda(ar, ai, ls_ref[d])
        den = ar * ar + ai * ai
        nr = lrr - 1.0
        lr, li = cols(lrr), cols(lir)
        f_re, f_im = cols((nr * ar + lir * ai) / den), cols((lir * ar - nr * ai) / den)
        b_re, b_im = br_ref[d], bi_ref[d]
        bb_re = f_re * b_re - f_im * b_im
        bb_im = f_re * b_im + f_im * b_re
        pr, pi = _cpow(lr, li, (n - 1 - lane_n) if d == 0 else lane_n, 4)
        lb_re = pr * bb_re - pi * bb_im
        lb_im = pr * bb_im + pi * bb_re
        qr, qi = _cpow(lr, li, (lane_n + 1) if d == 0 else (n - lane_n), 5)
        c_re, c_im = ctr_ref[d], cti_ref[d]
        base = 2 * d * LANE
        wpt_ref[:, base:base + p, :] = lb_re.astype(wpt_ref.dtype)
        wpt_ref[:, base + p:base + LANE, :] = zpad
        wpt_ref[:, base + LANE:base + LANE + p, :] = lb_im.astype(wpt_ref.dtype)
        wpt_ref[:, base + LANE + p:base + 2 * LANE, :] = zpad
        n_ref = nf_ref if d == 0 else nb_ref
        n_ref[:, 0:p, :] = (c_re * qr - c_im * qi).astype(n_ref.dtype)
        n_ref[:, p:LANE, :] = zpad
        n_ref[:, LANE:LANE + p, :] = (-(c_re * qi + c_im * qr)).astype(n_ref.dtype)
        n_ref[:, LANE + p:2 * LANE, :] = zpad
        taps.append(jnp.einsum('ghp,gpn->ghn', cr_ref[d], lb_re, precision=hp, preferred_element_type=F32)
                    - jnp.einsum('ghp,gpn->ghn', ci_ref[d], lb_im, precision=hp, preferred_element_type=F32))
        for _ in range(int(math.log2(n))):
            lrr, lir = lrr * lrr - lir * lir, 2.0 * lrr * lir
        for g in range(gb):
            l16_ref[g, 2 * d:2 * d + 1, :] = lrr[g:g + 1, :]
            l16_ref[g, 2 * d + 1:2 * d + 2, :] = lir[g:g + 1, :]
    for g in range(gb):
        kf, kb = taps[0][g], taps[1][g]
        blocks = []
        for t in range(n):
            f = kf if t == n - 1 else pltpu.roll(kf, w - (n - 1 - t) * S5_GROUP, 1)
            r = kb if t == 0 else pltpu.roll(kb, t * S5_GROUP, 1)
            blocks.append(jnp.where(lane_n2 <= t, f, 0.0) + jnp.where(lane_n2 >= t, r, 0.0))
        mt_ref[g] = jnp.concatenate(blocks, axis=0).astype(mt_ref.dtype)


def _s5w(ar_r, ai_r, ls_r, br_t, bi_t, ctr_t, cti_t, c_re, c_im):
    _, g, p, w = br_t.shape
    gb = LANE // S5_GROUP
    big = pl.BlockSpec((2, gb, p, w), lambda j: (0, j, 0, 0))
    cs = pl.BlockSpec((2, gb, S5_GROUP, p), lambda j: (0, j, 0, 0))
    rs = pl.BlockSpec((2, gb, LANE), lambda j: (0, j, 0))
    tab = lambda rows: pl.BlockSpec((gb, rows, w), lambda j: (j, 0, 0))
    return pl.pallas_call(
        _s5w_kernel,
        grid=(g // gb,),
        in_specs=[rs, rs, rs] + [big] * 4 + [cs, cs],
        out_specs=[tab(4 * LANE), tab(w), tab(2 * LANE), tab(2 * LANE), pl.BlockSpec((gb, 4, LANE), lambda j: (j, 0, 0))],
        out_shape=[jax.ShapeDtypeStruct((g, 4 * LANE, w), BF16), jax.ShapeDtypeStruct((g, w, w), BF16),
                   jax.ShapeDtypeStruct((g, 2 * LANE, w), BF16), jax.ShapeDtypeStruct((g, 2 * LANE, w), BF16),
                   jax.ShapeDtypeStruct((g, 4, LANE), F32)],
        compiler_params=_cparams(("parallel",)),
        name="s5w",
    )(ar_r, ai_r, ls_r, br_t, bi_t, ctr_t, cti_t, c_re, c_im)


def _block_transpose(xs, lane_blk):
    n = len(xs)
    rolled = []
    for dist in range(n):
        z = xs[0]
        for a in range(1, n):
            z = jnp.where(lane_blk == (a - dist) % n, xs[a], z)
        rolled.append(pltpu.roll(z, dist * S5_GROUP, 1) if dist else z)
    ys = []
    for b in range(n):
        y = rolled[(0 - b) % n]
        for a in range(1, n):
            y = jnp.where(lane_blk == a, rolled[(a - b) % n], y)
        ys.append(y)
    return ys


def _s5_kernel(n_ctx_chunks, u_ref, wpt_ref, mt_ref, nf_ref, nb_ref, l16_ref, d_ref, y_ref,
               u_scr, e_ref, sf_ref, sb_ref):
    gb = wpt_ref.shape[0]
    nbs = u_ref.shape[0]
    nck = u_ref.shape[1] // S5_CHUNK
    lanes = gb * nbs
    halves = MXU // LANE
    lane_blk = lax.broadcasted_iota(jnp.int32, (nck, LANE), 1) // S5_GROUP

    for b in range(nbs):
        for hf in range(halves):
            xs = [u_ref[b, pl.ds(hf * gb + s8, nck, stride=S5_CHUNK), :] for s8 in range(gb)]
            for g, ug in enumerate(_block_transpose(xs, lane_blk)):
                u_scr[g, b * nck:(b + 1) * nck, hf * LANE:(hf + 1) * LANE] = ug

    for g in range(gb):
        e = _mm(u_scr[g], wpt_ref[g], _NT)
        for b in range(nbs):
            for k in range(4):
                e_ref[k, pl.ds(g * nbs + b, nck, stride=lanes), :] = e[b * nck:(b + 1) * nck, k * LANE:(k + 1) * LANE]

    def lam_rows(k):
        return jnp.concatenate([jnp.broadcast_to(l16_ref[g, k:k + 1, :], (nbs, LANE)) for g in range(gb)], axis=0)

    lfr, lfi, lbr, lbi = (lam_rows(k) for k in range(4))
    zero = jnp.zeros((lanes, LANE), F32)

    def step(k, carry):
        fr, fi, br, bi = carry
        cb = jnp.where(k < n_ctx_chunks, n_ctx_chunks - 1 - k, nck - 1 + n_ctx_chunks - k)
        of = pl.multiple_of(k * lanes, lanes)
        ob = pl.multiple_of(cb * lanes, lanes)
        sf_ref[0, pl.ds(of, lanes), :] = fr
        sf_ref[1, pl.ds(of, lanes), :] = fi
        sb_ref[0, pl.ds(ob, lanes), :] = br
        sb_ref[1, pl.ds(ob, lanes), :] = bi
        efr, efi = e_ref[0, pl.ds(of, lanes), :], e_ref[1, pl.ds(of, lanes), :]
        ebr, ebi = e_ref[2, pl.ds(ob, lanes), :], e_ref[3, pl.ds(ob, lanes), :]
        return (lfr * fr - lfi * fi + efr, lfr * fi + lfi * fr + efi,
                lbr * br - lbi * bi + ebr, lbr * bi + lbi * br + ebi)

    lax.fori_loop(0, nck, step, (zero, zero, zero, zero))

    def rows_of(s_ref, g):
        return jnp.concatenate(
            [jnp.concatenate([s_ref[k, pl.ds(g * nbs + b, nck, stride=lanes), :] for k in range(2)], axis=1)
             for b in range(nbs)], axis=0)

    for g in range(gb):
        u = u_scr[g]
        y = _mm(u, mt_ref[g], _NT) + _mm(rows_of(sf_ref, g), nf_ref[g]) + _mm(rows_of(sb_ref, g), nb_ref[g])
        u_scr[g] = y + u * d_ref[g]

    for b in range(nbs):
        for hf in range(halves):
            ys = [u_scr[g, b * nck:(b + 1) * nck, hf * LANE:(hf + 1) * LANE] for g in range(gb)]
            for t8, yt in enumerate(_block_transpose(ys, lane_blk)):
                y_ref[b, pl.ds(hf * gb + t8, nck, stride=S5_CHUNK), :] = yt


def _s5(u, wpt, mt, nf, nbw, l16, d_t, n_ctx_chunks):
    b, tc, ds = u.shape
    g, _, w = mt.shape
    gb = LANE // S5_GROUP
    nbs = 2 if b % 2 == 0 else 1
    nck = tc // S5_CHUNK
    grp = lambda shape: pl.BlockSpec((gb,) + shape, lambda j, bh: (j,) + (0,) * len(shape))
    return pl.pallas_call(
        functools.partial(_s5_kernel, n_ctx_chunks),
        grid=(g // gb, b // nbs),
        in_specs=[pl.BlockSpec((nbs, tc, LANE), lambda j, bh: (bh, 0, j)),
                  grp((4 * LANE, w)), grp((w, w)), grp((2 * LANE, w)), grp((2 * LANE, w)), grp((4, LANE)),
                  grp((1, w))],
        out_specs=pl.BlockSpec((nbs, tc, LANE), lambda j, bh: (bh, 0, j)),
        out_shape=jax.ShapeDtypeStruct((b, tc, ds), F32),
        scratch_shapes=[pltpu.VMEM((gb, nbs * nck, w), F32),
                        pltpu.VMEM((4, nck * gb * nbs, LANE), F32),
                        pltpu.VMEM((2, nck * gb * nbs, LANE), F32),
                        pltpu.VMEM((2, nck * gb * nbs, LANE), F32)],
        compiler_params=_cparams(("parallel", "parallel")),
        name="s5",
    )(u, wpt, mt, nf, nbw, l16, d_t)


def _merge_kernel(yf_ref, yb_ref, bonus_ref, g_ref, y5_ref, x_ref, mod_ref, lnw_ref, lnb_ref, gluw_ref, glub_ref,
                  wout_ref, gffn_ref, x1_ref, h2_ref):
    ones = _head_ones()
    yh = yf_ref[...].astype(F32) + yb_ref[...].astype(F32)
    mean = _head_sum(yh, ones) * (1.0 / HEAD)
    cen = yh - mean
    var = _head_sum(cen * cen, ones) * (1.0 / HEAD)
    yn = cen * lax.rsqrt(var + LNX_EPS) * lnw_ref[...] + lnb_ref[...]
    o_rwkv = (yn + bonus_ref[...]) * g_ref[...]
    z = _gelu(y5_ref[...])
    o_s5 = z * _sigmoid(_mm(z, gluw_ref[...]) + glub_ref[...])
    o = jnp.concatenate([o_rwkv, o_s5], axis=1)
    mix = _mm(o, wout_ref[...])
    x1 = x_ref[...] + mod_ref[2:3, :] * mix
    x1_ref[...] = x1
    ms = jnp.mean(x1 * x1, axis=-1, keepdims=True)
    h = x1 * lax.rsqrt(ms + RMS_EPS) * gffn_ref[...]
    h2_ref[...] = (h * (1.0 + mod_ref[4:5, :]) + mod_ref[3:4, :]).astype(BF16)


def _merge(yf, yb, bonus, g, y5, x, mod3, lnx_w, lnx_b, glu_w, glu_b, w_out, g_ffn, n_ctx_tiles):
    b, t, d = x.shape
    dr = bonus.shape[-1]
    ds = y5.shape[-1]
    nt = t // TOK_TILE
    off = lambda bi, i: (bi, i + n_ctx_tiles, 0)
    const = lambda bi, i: (0, 0)
    return pl.pallas_call(
        _merge_kernel,
        grid=(b, nt),
        in_specs=[pl.BlockSpec((None, TOK_TILE, dr), off),
                  pl.BlockSpec((None, TOK_TILE, dr), off),
                  pl.BlockSpec((None, TOK_TILE, dr), off),
                  pl.BlockSpec((None, TOK_TILE, dr), off),
                  pl.BlockSpec((None, TOK_TILE, ds), off),
                  pl.BlockSpec((None, TOK_TILE, d), lambda bi, i: (bi, i, 0)),
                  pl.BlockSpec((None, 6, d), lambda bi, i: (bi, 0, 0)),
                  pl.BlockSpec((1, dr), const), pl.BlockSpec((1, dr), const),
                  pl.BlockSpec((ds, ds), const), pl.BlockSpec((1, ds), const),
                  pl.BlockSpec((dr + ds, d), const), pl.BlockSpec((1, d), const)],
        out_specs=[pl.BlockSpec((None, TOK_TILE, d), lambda bi, i: (bi, i, 0)),
                   pl.BlockSpec((None, TOK_TILE, d), lambda bi, i: (bi, i, 0))],
        out_shape=[jax.ShapeDtypeStruct((b, t, d), F32), jax.ShapeDtypeStruct((b, t, d), BF16)],
        compiler_params=_cparams(("parallel", "parallel")),
        name="merge",
    )(yf, yb, bonus, g, y5, x, mod3, lnx_w, lnx_b, glu_w, glu_b, w_out, g_ffn)


def _ffn_kernel(h_ref, wg_ref, wv_ref, cw_ref, cb_ref, wd_ref, x1_ref, mod_ref, gfin_ref, o_ref, acc_ref):
    f = pl.program_id(1)

    @pl.when(f == 0)
    def _():
        acc_ref[...] = jnp.zeros_like(acc_ref)

    h = h_ref[...]
    gate = jnp.dot(h, wg_ref[...], preferred_element_type=F32)
    val = jnp.dot(h, wv_ref[...], preferred_element_type=F32)
    prev, nxt = _shift_rows(gate)
    gate = cw_ref[0:1, :] * prev + cw_ref[1:2, :] * gate + cw_ref[2:3, :] * nxt + cb_ref[...]
    act = (_gelu(gate) * val).astype(BF16)
    acc_ref[...] += jnp.dot(act, wd_ref[...], preferred_element_type=F32)

    @pl.when(f == pl.num_programs(1) - 1)
    def _():
        x2 = x1_ref[...] + mod_ref[5:6, :] * acc_ref[...]
        ms = jnp.mean(x2 * x2, axis=-1, keepdims=True)
        o_ref[...] = x2 * lax.rsqrt(ms + RMS_EPS) * gfin_ref[...]


def _ffn(h2, w_up, conv_w, conv_b, w_down, x1, mod3, g_fin):
    b, t, d = x1.shape
    ff = w_down.shape[0]
    tm = min(FFN_TOK_TILE, t)
    tf = min(FFN_F_TILE, ff)
    nt, nf = t // tm, ff // tf
    return pl.pallas_call(
        _ffn_kernel,
        grid=(b * nt, nf),
        in_specs=[pl.BlockSpec((None, tm, d), lambda i, f: (i // nt, i % nt, 0)),
                  pl.BlockSpec((d, tf), lambda i, f: (0, f)),
                  pl.BlockSpec((d, tf), lambda i, f: (0, nf + f)),
                  pl.BlockSpec((3, tf), lambda i, f: (0, f)),
                  pl.BlockSpec((1, tf), lambda i, f: (0, f)),
                  pl.BlockSpec((tf, d), lambda i, f: (f, 0)),
                  pl.BlockSpec((None, tm, d), lambda i, f: (i // nt, i % nt, 0)),
                  pl.BlockSpec((None, 6, d), lambda i, f: (i // nt, 0, 0)),
                  pl.BlockSpec((1, d), lambda i, f: (0, 0))],
        out_specs=pl.BlockSpec((None, tm, d), lambda i, f: (i // nt, i % nt, 0)),
        out_shape=jax.ShapeDtypeStruct((b, t, d), F32),
        scratch_shapes=[pltpu.VMEM((tm, d), F32)],
        compiler_params=_cparams(("parallel", "arbitrary")),
        name="ffn",
    )(h2, w_up, w_up, conv_w, conv_b, w_down, x1, mod3, g_fin)


def _pad_cols(a, n):
    return jnp.pad(a, [(0, 0)] * (a.ndim - 1) + [(0, n - a.shape[-1])])


def _pack_in(a, dr, sizes, pads):
    gl, dl, al = sizes
    glp, wlp, alp = pads
    o = 3 * dr
    parts = [a[..., :o], _pad_cols(a[..., o:o + gl], glp), _pad_cols(a[..., o + gl:o + gl + 2 * dl], wlp),
             _pad_cols(a[..., o + gl + 2 * dl:o + gl + 2 * dl + 2 * al], alp), a[..., o + gl + 2 * dl + 2 * al:]]
    return jnp.concatenate(parts, axis=-1)


def _lora_up(w, rows):
    _, r, dr = w.shape
    out = jnp.zeros((rows, 2 * dr), w.dtype)
    out = out.at[0:r, 0:dr].set(w[0])
    out = out.at[r:2 * r, dr:2 * dr].set(w[1])
    return out


def kernel(x, c, ctx, c_ctx, mod_w, mod_b, norm_mix_g, w_in, shift_mu, rwkv_w0, rwkv_w_up, rwkv_a0, rwkv_a_up,
           rwkv_g_up, rwkv_k_k, rwkv_k_a, rwkv_r_k, lnx_w, lnx_b, s5_a_re, s5_a_im, s5_log_step, s5_b_re, s5_b_im,
           s5_c_re, s5_c_im, s5_d, s5_glu_w, s5_glu_b, w_out, norm_ffn_g, ffn_w_up, ffn_conv_w, ffn_conv_b,
           ffn_w_down, final_norm_g):
    assert mod_w.shape[0] == 1, "single trunk layer"
    b, t, d = x.shape
    ctx_len = ctx.shape[1]
    dr = rwkv_w0.shape[-1]
    ds = s5_d.shape[-1]
    dl, al, gl = rwkv_w_up.shape[2], rwkv_a_up.shape[2], rwkv_g_up.shape[1]
    n_groups = ds // S5_GROUP
    assert ctx_len == TOK_TILE and t % FFN_TOK_TILE == 0
    assert dr % MXU == 0 and s5_a_re.shape[-1] == S5_STATE and S5_CHUNK * S5_GROUP == MXU
    glp, wlp, alp = _round_up(gl, LANE), _round_up(2 * dl, LANE), _round_up(2 * al, LANE)
    dims = (dr, glp, wlp, alp)
    tc = ctx_len + t

    rows = _round_up(b + 1, 8)
    c_all = jnp.zeros((rows, d), F32).at[:b].set(c).at[b].set(c_ctx)
    mod3 = _mod(c_all, mod_w[0], mod_b).reshape(rows, 6, d)

    w_packed = _pack_in(w_in[0], dr, (gl, dl, al), (glp, wlp, alp)).astype(BF16)
    mu_p = _pack_in(shift_mu[0], dr, (gl, dl, al), (glp, wlp, alp))
    r, v, kk, lw, kd, bb, bonus, gate, u = _front(
        ctx, x, mod3, norm_mix_g, w_packed, mu_p, rwkv_w0[0], rwkv_a0[0], rwkv_k_k, rwkv_k_a,
        rwkv_r_k[0].reshape(1, dr), _lora_up(rwkv_w_up[0], wlp).astype(BF16),
        _lora_up(rwkv_a_up[0], alp).astype(BF16), jnp.pad(rwkv_g_up[0], [(0, glp - gl), (0, 0)]).astype(BF16),
        dims, ds, b)

    yf, yb = _wkv(r, v, kk, lw, kd, bb, ctx_len // WKV_CHUNK)

    w = S5_CHUNK * S5_GROUP
    b_t = lambda a: jnp.tile(a, (1, 1, 1, S5_CHUNK))
    c_t = lambda a: jnp.tile(a.transpose(0, 1, 3, 2), (1, 1, 1, S5_CHUNK))
    row = lambda a, fill: jnp.pad(a, [(0, 0), (0, 0), (0, LANE - S5_STATE)], constant_values=fill)
    ls_r = jnp.broadcast_to(s5_log_step[0][:, :, None], (2, n_groups, LANE))
    wpt, mt, nf, nbw, l16 = _s5w(row(s5_a_re[0], 1.0), row(s5_a_im[0], 0.0), ls_r, b_t(s5_b_re[0]), b_t(s5_b_im[0]),
                                 c_t(s5_c_re[0]), c_t(s5_c_im[0]), s5_c_re[0], s5_c_im[0])
    d_t = jnp.tile(s5_d[0].reshape(n_groups, 1, S5_GROUP), (1, 1, S5_CHUNK))
    y5 = _s5(u, wpt, mt, nf, nbw, l16, d_t, ctx_len // S5_CHUNK)

    x1, h2 = _merge(yf, yb, bonus, gate, y5, x, mod3, lnx_w, lnx_b, s5_glu_w[0].astype(BF16), s5_glu_b,
                    w_out[0].astype(BF16), norm_ffn_g, ctx_len // TOK_TILE)
    return _ffn(h2, ffn_w_up[0].astype(BF16), ffn_conv_w[0], ffn_conv_b, ffn_w_down[0].astype(BF16), x1, mod3,
                final_norm_g.reshape(1, d))
```
